```python
import math
import jax
import jax.numpy as jnp
from jax import lax
import numpy as np

D_MODEL = 2048
BATCH = 8
SEQ = 2048
DEPTH = 1
DEC_BATCH = 128
DEC_SEQ = 8
PAST_LEN = 2048
PAGE_SIZE = 128

MIX_WIDTH = D_MODEL
DA_HEADS = 4
DA_HEAD_DIM = 128
DA_V_DIM = 2 * DA_HEAD_DIM
DA_WIDTH = DA_HEADS * DA_V_DIM
DA_QK_COLS = DA_HEADS * 2 * DA_HEAD_DIM
CM_WIDTH = MIX_WIDTH - DA_WIDTH
CM_GROUPS = 8
CM_DIM = CM_WIDTH // CM_GROUPS
CHUNK = 128
IN_COLS = 2 * DA_QK_COLS + DA_WIDTH + 2 * CM_WIDTH
Q_BLOCK = 128
ROPE_THETA = 10000.0
N_EXPERT_GROUPS = 4
EXPERTS_PER_GROUP = 8
N_EXPERTS = N_EXPERT_GROUPS * EXPERTS_PER_GROUP
TOP_K = 2
D_EXPERT = D_MODEL // 4
RMS_EPS = 1e-6
LN_EPS = 1e-5

kernel_name = "hymba_diffattn_chunkgmlp_hmoe_step"


def _rms_norm(x, g):
    xf = x.astype(jnp.float32)
    y = xf * lax.rsqrt(jnp.mean(xf * xf, axis=-1, keepdims=True) + RMS_EPS)
    return (y * g.astype(jnp.float32)).astype(x.dtype)


def _layer_norm(x, g, b):
    xf = x.astype(jnp.float32)
    xc = xf - jnp.mean(xf, axis=-1, keepdims=True)
    var = jnp.mean(xc * xc, axis=-1, keepdims=True)
    y = xc * lax.rsqrt(var + LN_EPS) * g.astype(jnp.float32) + b.astype(jnp.float32)
    return y.astype(x.dtype)


def _rope(x, pos):
    half = DA_HEAD_DIM // 2
    inv_freq = jnp.exp(-math.log(ROPE_THETA) * jnp.arange(half, dtype=jnp.float32) * (2.0 / DA_HEAD_DIM))
    ang = pos[:, None] * inv_freq[None, :]
    cos = jnp.cos(ang)[None, :, None, None, :]
    sin = jnp.sin(ang)[None, :, None, None, :]
    xf = x.astype(jnp.float32)
    x1, x2 = xf[..., :half], xf[..., half:]
    out = jnp.concatenate([x1 * cos - x2 * sin, x2 * cos + x1 * sin], axis=-1)
    return out.astype(x.dtype)


def _split_projection(z, ln_g, ln_b):
    B, T, _ = z.shape
    q = z[..., :DA_QK_COLS].reshape(B, T, DA_HEADS, 2, DA_HEAD_DIM)
    k = z[..., DA_QK_COLS:2 * DA_QK_COLS].reshape(B, T, DA_HEADS, 2, DA_HEAD_DIM)
    v = z[..., 2 * DA_QK_COLS:2 * DA_QK_COLS + DA_WIDTH].reshape(B, T, DA_HEADS, DA_V_DIM)
    cm = jax.nn.gelu(z[..., 2 * DA_QK_COLS + DA_WIDTH:], approximate=False)
    u = cm[..., :CM_WIDTH]
    vg = _layer_norm(cm[..., CM_WIDTH:], ln_g, ln_b)
    return q, k, v, u, vg


def _diff_lambda(lq1, lk1, lq2, lk2, lam_init):
    f = lambda a: a.astype(jnp.float32)
    return jnp.exp(jnp.sum(f(lq1) * f(lk1))) - jnp.exp(jnp.sum(f(lq2) * f(lk2))) + lam_init


def _diff_probs(s, lam):
    p = jax.nn.softmax(s, axis=-1)
    return p[:, :, 0] - lam * p[:, :, 1]


def _diff_attn_prompt(q, k, v, lam):
    B, S = q.shape[:2]
    nqb = S // Q_BLOCK
    scale = DA_HEAD_DIM ** -0.5
    qb = jnp.moveaxis(q.reshape(B, nqb, Q_BLOCK, DA_HEADS, 2, DA_HEAD_DIM), 1, 0)
    k_pos = jnp.arange(S)

    def block(args):
        q_blk, i = args
        s = jnp.einsum('bqhcd,bkhcd->bhcqk', q_blk, k).astype(jnp.float32) * scale
        q_pos = i * Q_BLOCK + jnp.arange(Q_BLOCK)
        s = jnp.where(k_pos[None, :] <= q_pos[:, None], s, -jnp.inf)
        a = _diff_probs(s, lam).astype(v.dtype)
        return jnp.einsum('bhqk,bkhe->bqhe', a, v)

    out = lax.map(block, (qb, jnp.arange(nqb)))
    return jnp.moveaxis(out, 0, 1).reshape(B, S, DA_HEADS, DA_V_DIM)


def _diff_attn_sample(q, k_new, v_new, k_past, v_past, lam):
    T = q.shape[1]
    P = k_past.shape[1]
    scale = DA_HEAD_DIM ** -0.5
    s_past = jnp.einsum('bqhcd,bkhcd->bhcqk', q, k_past).astype(jnp.float32) * scale
    s_new = jnp.einsum('bqhcd,bkhcd->bhcqk', q, k_new).astype(jnp.float32) * scale
    causal = jnp.tril(jnp.ones((T, T), dtype=bool))
    s_new = jnp.where(causal, s_new, -jnp.inf)
    a = _diff_probs(jnp.concatenate([s_past, s_new], axis=-1), lam).astype(v_new.dtype)
    return (jnp.einsum('bhqk,bkhe->bqhe', a[..., :P], v_past)
            + jnp.einsum('bhqk,bkhe->bqhe', a[..., P:], v_new))


def _post_attn(o, g_subln, lam_init):
    B, T = o.shape[:2]
    return (_rms_norm(o, g_subln) * (1.0 - lam_init)).reshape(B, T, DA_WIDTH)


def _chunk_mix_prompt(u, vg, w_s, b_s):
    B, S, _ = vg.shape
    nc = S // CHUNK
    vr = vg.reshape(B, nc, CHUNK, CM_GROUPS, CM_DIM)
    wm = jnp.tril(w_s).astype(vg.dtype)
    s = jnp.einsum('gij,bcjgd->bcigd', wm, vr) + b_s.T.astype(vg.dtype)[:, :, None]
    return (u.reshape(B, nc, CHUNK, CM_GROUPS, CM_DIM) * s).reshape(B, S, CM_WIDTH)


def _chunk_mix_sample(u, vg, w_s, b_s):
    Bd, T, _ = vg.shape
    vr = vg.reshape(Bd, T, CM_GROUPS, CM_DIM)
    wm = jnp.tril(w_s)[:, :T, :T].astype(vg.dtype)
    s = jnp.einsum('gij,bjgd->bigd', wm, vr) + b_s[:, :T].T.astype(vg.dtype)[:, :, None]
    return (u.reshape(Bd, T, CM_GROUPS, CM_DIM) * s).reshape(Bd, T, CM_WIDTH)


def _hier_moe(h, w_rg, b_rg, w_re, b_re, w1, w3, w2):
    N = h.shape[0]
    g_logits = (h @ w_rg).astype(jnp.float32) + b_rg.astype(jnp.float32)
    g_prob = jax.nn.softmax(g_logits, axis=-1)
    g_idx = jnp.argmax(g_logits, axis=-1)
    g_w = jnp.take_along_axis(g_prob, g_idx[:, None], axis=-1)
    e_logits = ((h @ w_re).astype(jnp.float32) + b_re.astype(jnp.float32)).reshape(
        N, N_EXPERT_GROUPS, EXPERTS_PER_GROUP)
    e_logits = jnp.take_along_axis(e_logits, g_idx[:, None, None], axis=1)[:, 0]
    top_v, top_i = lax.top_k(e_logits, TOP_K)
    top_w = jax.nn.softmax(top_v, axis=-1) * g_w
    e_id = g_idx[:, None] * EXPERTS_PER_GROUP + top_i
    gate = jnp.einsum('nk,nke->ne', top_w,
                      jax.nn.one_hot(e_id, N_EXPERTS, dtype=jnp.float32)).astype(h.dtype)
    y = jnp.zeros_like(h)
    for e in range(N_EXPERTS):
        a = jax.nn.silu(h @ w1[e]) * (h @ w3[e])
        y = y + (a * gate[:, e:e + 1]) @ w2[e]
    return y


def setup_inputs(seed: int = 0) -> dict:
    key = jax.random.key(seed)
    ks = jax.random.split(key, 26)
    n_pages = PAST_LEN // PAGE_SIZE
    n_phys = (DEC_BATCH * n_pages * 5 + 3) // 4
    f32 = jnp.float32
    nrm = lambda k, shape, scale: jax.random.normal(k, shape, f32) * scale
    gain = lambda k, shape: 1.0 + 0.02 * jax.random.normal(k, shape, f32)
    page_table = jax.random.permutation(ks[4], n_phys)[:DEC_BATCH * n_pages]
    page_table = page_table.reshape(DEC_BATCH, n_pages).astype(jnp.int32)
    return {
        "x_prompt": nrm(ks[0], (BATCH, SEQ, D_MODEL), 1.0),
        "x_sample": nrm(ks[1], (DEC_BATCH, DEC_SEQ, D_MODEL), 1.0),
        "cache_k": nrm(ks[2], (DEPTH, n_phys, PAGE_SIZE, DA_HEADS, 2, DA_HEAD_DIM), 1.0),
        "cache_v": nrm(ks[3], (DEPTH, n_phys, PAGE_SIZE, DA_HEADS, DA_V_DIM), 1.0),
        "page_table": page_table,
        "g_attn": gain(ks[5], (DEPTH, D_MODEL)),
        "w_in": nrm(ks[6], (DEPTH, D_MODEL, IN_COLS), D_MODEL ** -0.5),
        "lambda_q1": nrm(ks[7], (DEPTH, DA_HEAD_DIM), 0.1),
        "lambda_k1": nrm(ks[8], (DEPTH, DA_HEAD_DIM), 0.1),
        "lambda_q2": nrm(ks[9], (DEPTH, DA_HEAD_DIM), 0.1),
        "lambda_k2": nrm(ks[10], (DEPTH, DA_HEAD_DIM), 0.1),
        "g_subln": gain(ks[11], (DEPTH, DA_V_DIM)),
        "ln_v_g": gain(ks[12], (DEPTH, CM_WIDTH)),
        "ln_v_b": nrm(ks[13], (DEPTH, CM_WIDTH), 0.02),
        "w_spatial": nrm(ks[14], (DEPTH, CM_GROUPS, CHUNK, CHUNK), 0.5 * CHUNK ** -0.5),
        "b_spatial": gain(ks[15], (DEPTH, CM_GROUPS, CHUNK)),
        "w_out": nrm(ks[16], (DEPTH, MIX_WIDTH, D_MODEL), MIX_WIDTH ** -0.5),
        "g_ffn": gain(ks[17], (DEPTH, D_MODEL)),
        "w_router_grp": nrm(ks[18], (DEPTH, D_MODEL, N_EXPERT_GROUPS), D_MODEL ** -0.5),
        "b_router_grp": nrm(ks[19], (DEPTH, N_EXPERT_GROUPS), 0.01),
        "w_router_exp": nrm(ks[20], (DEPTH, D_MODEL, N_EXPERTS), D_MODEL ** -0.5),
        "b_router_exp": nrm(ks[21], (DEPTH, N_EXPERTS), 0.01),
        "w1": nrm(ks[22], (DEPTH, N_EXPERTS, D_MODEL, D_EXPERT), D_MODEL ** -0.5),
        "w3": nrm(ks[23], (DEPTH, N_EXPERTS, D_MODEL, D_EXPERT), D_MODEL ** -0.5),
        "w2": nrm(ks[24], (DEPTH, N_EXPERTS, D_EXPERT, D_MODEL), D_EXPERT ** -0.5),
        "g_final": gain(ks[25], (D_MODEL,)),
    }


def reference(x_prompt, x_sample, cache_k, cache_v, page_table, g_attn, w_in,
              lambda_q1, lambda_k1, lambda_q2, lambda_k2, g_subln, ln_v_g, ln_v_b,
              w_spatial, b_spatial, w_out, g_ffn, w_router_grp, b_router_grp,
              w_router_exp, b_router_exp, w1, w3, w2, g_final):
    B, S, _ = x_prompt.shape
    Bd, T, _ = x_sample.shape
    past = page_table.shape[1] * PAGE_SIZE
    pos_p = jnp.arange(S, dtype=jnp.float32)
    pos_s = past + jnp.arange(T, dtype=jnp.float32)
    xp, xs = x_prompt, x_sample
    kp_l, vp_l, ks_l, vs_l, cvp_l, cvs_l = [], [], [], [], [], []
    for l in range(DEPTH):
        lam_init = 0.8 - 0.6 * math.exp(-0.3 * l)
        lam = _diff_lambda(lambda_q1[l], lambda_k1[l], lambda_q2[l], lambda_k2[l], lam_init)

        zp = _rms_norm(xp, g_attn[l]) @ w_in[l]
        q, k, v, u, vg = _split_projection(zp, ln_v_g[l], ln_v_b[l])
        q = _rope(q, pos_p)
        k = _rope(k, pos_p)
        att = _post_attn(_diff_attn_prompt(q, k, v, lam), g_subln[l], lam_init)
        cm = _chunk_mix_prompt(u, vg, w_spatial[l], b_spatial[l])
        xp = xp + jnp.concatenate([att, cm], axis=-1) @ w_out[l]
        kp_l.append(k)
        vp_l.append(v)
        cvp_l.append(vg[:, S - CHUNK:])

        zs = _rms_norm(xs, g_attn[l]) @ w_in[l]
        q, k, v, u, vg = _split_projection(zs, ln_v_g[l], ln_v_b[l])
        q = _rope(q, pos_s)
        k = _rope(k, pos_s)
        k_past = cache_k[l][page_table].reshape(Bd, past, DA_HEADS, 2, DA_HEAD_DIM)
        v_past = cache_v[l][page_table].reshape(Bd, past, DA_HEADS, DA_V_DIM)
        att = _post_attn(_diff_attn_sample(q, k, v, k_past, v_past, lam), g_subln[l], lam_init)
        cm = _chunk_mix_sample(u, vg, w_spatial[l], b_spatial[l])
        xs = xs + jnp.concatenate([att, cm], axis=-1) @ w_out[l]
        ks_l.append(k)
        vs_l.append(v)
        cvs_l.append(vg)

        h = jnp.concatenate([_rms_norm(xp, g_ffn[l]).reshape(B * S, D_MODEL),
                             _rms_norm(xs, g_ffn[l]).reshape(Bd * T, D_MODEL)], axis=0)
        y = _hier_moe(h, w_router_grp[l], b_router_grp[l], w_router_exp[l], b_router_exp[l],
                      w1[l], w3[l], w2[l])
        xp = xp + y[:B * S].reshape(B, S, D_MODEL)
        xs = xs + y[B * S:].reshape(Bd, T, D_MODEL)

    y_prompt = _rms_norm(xp, g_final)
    y_sample = _rms_norm(xs, g_final)
    return (y_prompt, y_sample, jnp.stack(kp_l), jnp.stack(vp_l), jnp.stack(ks_l),
            jnp.stack(vs_l), jnp.stack(cvp_l), jnp.stack(cvs_l))
```

```python
import functools
import math

import jax
import jax.numpy as jnp
from jax import lax
from jax.experimental import pallas as pl
from jax.experimental.pallas import tpu as pltpu

F32 = jnp.float32
BF16 = jnp.bfloat16

ROPE_THETA = 10000.0
RMS_EPS = 1e-6
LN_EPS = 1e-5
TOP_K = 2

LANES = 128
SLAB = 16
VMEM_LIMIT = 52 * 1024 * 1024

NT_DIMS = (((1,), (1,)), ((), ()))


def _params(n_axes, vmem=VMEM_LIMIT):
    return pltpu.CompilerParams(dimension_semantics=("arbitrary",) * n_axes, vmem_limit_bytes=vmem)


def _rms(x, g):
    return x * lax.rsqrt(jnp.mean(x * x, axis=-1, keepdims=True) + RMS_EPS) * g


def _rope_table_kernel(cos_ref, sin_ref, *, period, offset, head_dim):
    rows, lanes = cos_ref.shape
    half = head_dim // 2
    row = lax.broadcasted_iota(jnp.int32, (rows, lanes), 0)
    lane = lax.broadcasted_iota(jnp.int32, (rows, lanes), 1)
    pos = (offset + row % period).astype(F32)
    j = (lane % half).astype(F32)
    inv_freq = jnp.exp(-math.log(ROPE_THETA) * j * (2.0 / head_dim))
    ang = pos * inv_freq
    cos_ref[...] = jnp.cos(ang)
    s = jnp.sin(ang)
    sin_ref[...] = jnp.where(lane < half, -s, s)


def _rope_tables(rows, period, offset, head_dim):
    return pl.pallas_call(
        functools.partial(_rope_table_kernel, period=period, offset=offset, head_dim=head_dim),
        out_shape=(jax.ShapeDtypeStruct((rows, head_dim), F32),) * 2,
        name="rope_tables",
    )()


def _in_proj_kernel(x_ref, g_ref, w_ref, cos_ref, sin_ref, lng_ref, lnb_ref,
                    q_ref, k_ref, v_ref, u_ref, vg_ref, xn_ref, *, q_scale, head_dim):
    j = pl.program_id(1)

    @pl.when(j == 0)
    def _():
        xn_ref[...] = _rms(x_ref[...], g_ref[...]).astype(BF16)

    z = jnp.dot(xn_ref[...], w_ref[...], preferred_element_type=F32)

    def rope_into(out_ref, scale):
        cos = cos_ref[...]
        sin = sin_ref[...]
        for c in range(z.shape[1] // head_dim):
            zc = z[:, c * head_dim:(c + 1) * head_dim]
            r = zc * cos + pltpu.roll(zc, head_dim // 2, axis=1) * sin
            if scale != 1.0:
                r = r * scale
            out_ref[:, c * head_dim:(c + 1) * head_dim] = r.astype(out_ref.dtype)

    def gelu(t):
        return 0.5 * t * (1.0 + lax.erf(t * math.sqrt(0.5)))

    @pl.when(j == 0)
    def _():
        rope_into(q_ref, q_scale)

    @pl.when(j == 1)
    def _():
        rope_into(k_ref, 1.0)

    @pl.when(j == 2)
    def _():
        v_ref[...] = z

    @pl.when(j == 3)
    def _():
        u_ref[...] = gelu(z).astype(u_ref.dtype)

    @pl.when(j == 4)
    def _():
        c = gelu(z)
        cc = c - jnp.mean(c, axis=-1, keepdims=True)
        var = jnp.mean(cc * cc, axis=-1, keepdims=True)
        vg_ref[...] = cc * lax.rsqrt(var + LN_EPS) * lng_ref[...] + lnb_ref[...]


def _in_projection(x2d, g, w_bf, cos_t, sin_t, ln_g, ln_b, *, tm, table_blocks, q_dtype, head_dim, name):
    m, d = x2d.shape
    sec = w_bf.shape[1] // 5
    row_blk = lambda i, j: (i, 0)
    fixed = lambda i, j: (0, 0)
    table = lambda i, j: (i % table_blocks, 0)
    out_dtypes = (q_dtype, F32, F32, BF16, F32)
    return pl.pallas_call(
        functools.partial(_in_proj_kernel, q_scale=head_dim ** -0.5, head_dim=head_dim),
        grid=(m // tm, 5),
        in_specs=[
            pl.BlockSpec((tm, d), row_blk),
            pl.BlockSpec((1, d), fixed),
            pl.BlockSpec((d, sec), lambda i, j: (0, j)),
            pl.BlockSpec((tm, head_dim), table),
            pl.BlockSpec((tm, head_dim), table),
            pl.BlockSpec((1, sec), fixed),
            pl.BlockSpec((1, sec), fixed),
        ],
        out_specs=[pl.BlockSpec((tm, sec), row_blk)] * 5,
        out_shape=[jax.ShapeDtypeStruct((m, sec), dt) for dt in out_dtypes],
        scratch_shapes=[pltpu.VMEM((tm, d), BF16)],
        compiler_params=_params(2),
        name=name,
    )(x2d, g, w_bf, cos_t, sin_t, ln_g, ln_b)


def _diff_lambda(lq1_ref, lk1_ref, lq2_ref, lk2_ref, lam_init):
    a = jnp.sum(lq1_ref[...] * lk1_ref[...], axis=-1, keepdims=True)
    b = jnp.sum(lq2_ref[...] * lk2_ref[...], axis=-1, keepdims=True)
    return jnp.exp(a) - jnp.exp(b) + lam_init


def _prompt_attn_kernel(lq1_ref, lk1_ref, lq2_ref, lk2_ref, gsub_ref, q_ref, k_ref, v_ref, o_ref,
                        kb_ref, vb_ref, m_ref, l_ref, acc_ref, *, tq, head_dim, lam_init):
    qi = pl.program_id(2)

    @pl.when(qi == 0)
    def _():
        kb_ref[...] = k_ref[...].astype(BF16)
        vb_ref[...] = v_ref[...].astype(BF16)

    m_ref[...] = jnp.full(m_ref.shape, -jnp.inf, F32)
    l_ref[...] = jnp.zeros(l_ref.shape, F32)
    acc_ref[...] = jnp.zeros(acc_ref.shape, F32)
    q = q_ref[...]

    def block(kb, masked):
        start = pl.multiple_of(kb * tq, tq)
        ks = kb_ref[pl.ds(start, tq), :]
        vs = vb_ref[pl.ds(start, tq), :]
        for c in range(2):
            sl = slice(c * head_dim, (c + 1) * head_dim)
            s = lax.dot_general(q[:, sl], ks[:, sl], NT_DIMS, preferred_element_type=F32)
            if masked:
                row = lax.broadcasted_iota(jnp.int32, s.shape, 0)
                col = lax.broadcasted_iota(jnp.int32, s.shape, 1)
                s = jnp.where(col <= row, s, -jnp.inf)
            m_prev = m_ref[c]
            m_new = jnp.maximum(m_prev, jnp.max(s, axis=-1, keepdims=True))
            alpha = jnp.exp(m_prev - m_new)
            p = jnp.exp(s - m_new)
            l_ref[c] = alpha * l_ref[c] + jnp.sum(p, axis=-1, keepdims=True)
            acc_ref[c] = alpha * acc_ref[c] + jnp.dot(p.astype(BF16), vs, preferred_element_type=F32)
            m_ref[c] = m_new

    def body(kb, carry):
        block(kb, False)
        return carry

    lax.fori_loop(0, qi, body, 0)
    block(qi, True)

    lam = _diff_lambda(lq1_ref, lk1_ref, lq2_ref, lk2_ref, lam_init)
    o = acc_ref[0] / l_ref[0] - lam * (acc_ref[1] / l_ref[1])
    o_ref[...] = (_rms(o, gsub_ref[...]) * (1.0 - lam_init)).astype(o_ref.dtype)


def _prompt_attention(lams, g_sub, q2d, k2d, v2d, *, batch, seq, heads, head_dim, tq, lam_init):
    nq = seq // tq
    hw = 2 * head_dim
    vec = pl.BlockSpec((1, head_dim), lambda b, h, i: (0, 0))
    q_blk = pl.BlockSpec((tq, hw), lambda b, h, i: (b * nq + i, h))
    kv_blk = pl.BlockSpec((seq, hw), lambda b, h, i: (b, h))
    return pl.pallas_call(
        functools.partial(_prompt_attn_kernel, tq=tq, head_dim=head_dim, lam_init=lam_init),
        grid=(batch, heads, nq),
        in_specs=[vec, vec, vec, vec, pl.BlockSpec((1, hw), lambda b, h, i: (0, 0)), q_blk, kv_blk, kv_blk],
        out_specs=q_blk,
        out_shape=jax.ShapeDtypeStruct(q2d.shape, BF16),
        scratch_shapes=[
            pltpu.VMEM((seq, hw), BF16), pltpu.VMEM((seq, hw), BF16),
            pltpu.VMEM((2, tq, 1), F32), pltpu.VMEM((2, tq, 1), F32), pltpu.VMEM((2, tq, hw), F32),
        ],
        compiler_params=_params(3),
        name="prompt_attention",
    )(*lams, g_sub, q2d, k2d, v2d)


def _sample_attn_kernel(pt_ref, lq1_ref, lk1_ref, lq2_ref, lk2_ref, gsub_ref, q_ref, kn_ref, vn_ref, *rest,
                        n_pages, page, heads, head_dim, lam_init):
    kc = rest[:n_pages]
    vc = rest[n_pages:2 * n_pages]
    o_ref, s_ref, knp_ref, vnp_ref = rest[2 * n_pages:]
    t = q_ref.shape[0]
    hw = 2 * head_dim

    @pl.when(pl.program_id(0) == 0)
    def _():
        knp_ref[...] = jnp.zeros(knp_ref.shape, F32)
        vnp_ref[...] = jnp.zeros(vnp_ref.shape, F32)

    knp_ref[0:t, :] = kn_ref[...]
    vnp_ref[0:t, :] = vn_ref[...]

    q = q_ref[...].astype(BF16)
    for j in range(n_pages + 1):
        kj = (kc[j][...] if j < n_pages else knp_ref[...]).astype(BF16)
        for hc in range(2 * heads):
            sl = slice(hc * head_dim, (hc + 1) * head_dim)
            s = lax.dot_general(q[:, sl], kj[:, sl], NT_DIMS, preferred_element_type=F32)
            if j == n_pages:
                row = lax.broadcasted_iota(jnp.int32, s.shape, 0)
                col = lax.broadcasted_iota(jnp.int32, s.shape, 1)
                s = jnp.where(col <= row, s, -jnp.inf)
            s_ref[hc * t:(hc + 1) * t, j * page:(j + 1) * page] = s

    s = s_ref[...]
    p = jnp.exp(s - jnp.max(s, axis=-1, keepdims=True))
    a = p / jnp.sum(p, axis=-1, keepdims=True)
    lam = _diff_lambda(lq1_ref, lk1_ref, lq2_ref, lk2_ref, lam_init)
    for h in range(heads):
        a_h = (a[2 * h * t:(2 * h + 1) * t] - lam * a[(2 * h + 1) * t:(2 * h + 2) * t]).astype(BF16)
        acc = jnp.zeros((t, hw), F32)
        for j in range(n_pages + 1):
            src = vc[j] if j < n_pages else vnp_ref
            vj = src[:, h * hw:(h + 1) * hw].astype(BF16)
            acc = acc + jnp.dot(a_h[:, j * page:(j + 1) * page], vj, preferred_element_type=F32)
        o_ref[:, h * hw:(h + 1) * hw] = _rms(acc, gsub_ref[...]) * (1.0 - lam_init)


def _sample_attention(page_table, lams, g_sub, q2d, k2d, v2d, cache_k4, cache_v4, *, layer, dec_batch, t,
                      heads, head_dim, lam_init):
    n_pages = page_table.shape[1]
    page, width = cache_k4.shape[2], cache_k4.shape[3]
    hw = 2 * head_dim
    vec = pl.BlockSpec((1, head_dim), lambda b, pt: (0, 0))
    row_blk = pl.BlockSpec((t, width), lambda b, pt: (b, 0))

    def page_spec(j):
        return pl.BlockSpec((None, None, page, width), lambda b, pt: (layer, pt[b * n_pages + j], 0, 0))

    pages = [page_spec(j) for j in range(n_pages)]
    grid_spec = pltpu.PrefetchScalarGridSpec(
        num_scalar_prefetch=1,
        grid=(dec_batch,),
        in_specs=[vec, vec, vec, vec, pl.BlockSpec((1, hw), lambda b, pt: (0, 0)), row_blk, row_blk, row_blk]
        + pages + pages,
        out_specs=row_blk,
        scratch_shapes=[
            pltpu.VMEM((2 * heads * t, (n_pages + 1) * page), F32),
            pltpu.VMEM((page, width), F32), pltpu.VMEM((page, width), F32),
        ],
    )
    return pl.pallas_call(
        functools.partial(_sample_attn_kernel, n_pages=n_pages, page=page, heads=heads, head_dim=head_dim,
                          lam_init=lam_init),
        grid_spec=grid_spec,
        out_shape=jax.ShapeDtypeStruct(q2d.shape, F32),
        compiler_params=_params(1),
        name="sample_attention",
    )(page_table.reshape(-1), *lams, g_sub, q2d, k2d, v2d, *([cache_k4] * n_pages), *([cache_v4] * n_pages))


def _mix_out_kernel(x_ref, att_ref, u_ref, vg_ref, wm_ref, bs_ref, wo_ref, gffn_ref, wr_ref, br_ref, *rest,
                    blk, chunk, aliased):
    x1_ref, h3_ref, lg_ref, cm_ref = rest[3:] if aliased else rest
    tm = x_ref.shape[0]
    groups = wm_ref.shape[0]
    da_width = att_ref.shape[1]
    row = lax.broadcasted_iota(jnp.int32, (chunk, chunk), 0)
    col = lax.broadcasted_iota(jnp.int32, (chunk, chunk), 1)
    keep = (row // blk == col // blk) & (col <= row)
    for g in range(groups):
        gs = slice(g * chunk, (g + 1) * chunk)
        wm = jnp.where(keep, wm_ref[g], 0.0).astype(BF16)
        for c in range(tm // chunk):
            rs = slice(c * chunk, (c + 1) * chunk)
            s = jnp.dot(wm, vg_ref[rs, gs].astype(BF16), preferred_element_type=F32) + bs_ref[g]
            cm_ref[rs, gs] = (u_ref[rs, gs].astype(F32) * s).astype(BF16)
    y = jnp.dot(att_ref[...].astype(BF16), wo_ref[0:da_width, :], preferred_element_type=F32)
    y = y + jnp.dot(cm_ref[...], wo_ref[da_width:, :], preferred_element_type=F32)
    x1 = x_ref[...] + y
    x1_ref[...] = x1
    h = _rms(x1, gffn_ref[...])
    for j in range(SLAB):
        h3_ref[pl.ds(j, tm, stride=SLAB), :] = h[:, j * LANES:(j + 1) * LANES]
    lg_ref[...] = jnp.dot(h, wr_ref[...], preferred_element_type=F32,
                          precision=lax.Precision.HIGHEST) + br_ref[...]


def _mix_out(x2d, att, u, vg, wm, bs, wo_bf, g_ffn, wr, br, prev, *, tm, row_offset, n_total, blk, chunk, name):
    m, d = x2d.shape
    width = att.shape[1]
    groups = wm.shape[0]
    off = row_offset // tm
    row_blk = lambda w: pl.BlockSpec((tm, w), lambda i: (i, 0))
    fixed2 = lambda a: pl.BlockSpec(a.shape, lambda i: (0, 0))
    fixed3 = lambda a: pl.BlockSpec(a.shape, lambda i: (0, 0, 0))
    in_specs = [row_blk(d), row_blk(width), row_blk(width), row_blk(width), fixed3(wm), fixed3(bs),
                fixed2(wo_bf), fixed2(g_ffn), fixed2(wr), fixed2(br)]
    args = [x2d, att, u, vg, wm, bs, wo_bf, g_ffn, wr, br]
    aliases = {}
    if prev is not None:
        in_specs += [pl.BlockSpec(memory_space=pl.ANY)] * 3
        aliases = {len(args) + n: n for n in range(3)}
        args += list(prev)
    return pl.pallas_call(
        functools.partial(_mix_out_kernel, blk=blk, chunk=chunk, aliased=prev is not None),
        grid=(m // tm,),
        in_specs=in_specs,
        out_specs=[
            pl.BlockSpec((tm, d), lambda i: (i + off, 0)),
            pl.BlockSpec((tm * SLAB, LANES), lambda i: (i + off, 0)),
            pl.BlockSpec((tm, LANES), lambda i: (i + off, 0)),
        ],
        out_shape=[
            jax.ShapeDtypeStruct((n_total, d), F32),
            jax.ShapeDtypeStruct((n_total * SLAB, LANES), F32),
            jax.ShapeDtypeStruct((n_total, LANES), F32),
        ],
        scratch_shapes=[pltpu.VMEM((tm, groups * chunk), BF16)],
        input_output_aliases=aliases,
        compiler_params=_params(1),
        name=name,
    )(*args)


def _route_kernel(lg_ref, ri_ref, rw_ref, cnt_ref, carry_ref, *, n_groups, epg, exp_row0):
    tm = lg_ref.shape[0]
    n_exp = n_groups * epg

    @pl.when(pl.program_id(0) == 0)
    def _():
        carry_ref[...] = jnp.zeros(carry_ref.shape, F32)

    lt = lg_ref[...].T
    gl = lt[0:n_groups]
    gmax = jnp.max(gl, axis=0, keepdims=True)
    sub_g = lax.broadcasted_iota(jnp.int32, gl.shape, 0)
    gidx = jnp.min(jnp.where(gl == gmax, sub_g, n_groups), axis=0, keepdims=True)
    gw = 1.0 / jnp.sum(jnp.exp(gl - gmax), axis=0, keepdims=True)
    esel = jnp.zeros((epg, tm), F32)
    for gi in range(n_groups):
        esel = jnp.where(gidx == gi, lt[exp_row0 + gi * epg:exp_row0 + (gi + 1) * epg], esel)
    sub_e = lax.broadcasted_iota(jnp.int32, esel.shape, 0)
    v1 = jnp.max(esel, axis=0, keepdims=True)
    i1 = jnp.min(jnp.where(esel == v1, sub_e, epg), axis=0, keepdims=True)
    rest = jnp.where(sub_e == i1, -jnp.inf, esel)
    v2 = jnp.max(rest, axis=0, keepdims=True)
    i2 = jnp.min(jnp.where(rest == v2, sub_e, epg), axis=0, keepdims=True)
    tt = jnp.exp(v2 - v1)
    w1 = gw / (1.0 + tt)
    w2 = gw * tt / (1.0 + tt)
    e1 = gidx * epg + i1
    e2 = gidx * epg + i2

    sub_x = lax.broadcasted_iota(jnp.int32, (n_exp, tm), 0)
    oh1 = sub_x == e1
    oh2 = sub_x == e2
    oh = jnp.where(oh1, 1.0, 0.0) + jnp.where(oh2, 1.0, 0.0)
    earlier = lax.broadcasted_iota(jnp.int32, (tm, tm), 0) < lax.broadcasted_iota(jnp.int32, (tm, tm), 1)
    before = jnp.dot(oh.astype(BF16), jnp.where(earlier, 1.0, 0.0).astype(BF16),
                     preferred_element_type=F32) + carry_ref[...]
    r1 = jnp.sum(jnp.where(oh1, before, 0.0), axis=0, keepdims=True)
    r2 = jnp.sum(jnp.where(oh2, before, 0.0), axis=0, keepdims=True)
    carry_ref[...] = carry_ref[...] + jnp.sum(oh, axis=1, keepdims=True)

    zi = jnp.zeros((4, tm), jnp.int32)
    ri_ref[...] = jnp.concatenate([e1, e2, r1.astype(jnp.int32), r2.astype(jnp.int32), zi], axis=0)
    rw_ref[...] = jnp.concatenate([w1, w2, jnp.zeros((6, tm), F32)], axis=0)
    cnt_ref[...] = jnp.broadcast_to(carry_ref[...], cnt_ref.shape)


def _route(logits, *, tm, n_groups, epg, exp_row0):
    n = logits.shape[0]
    n_exp = n_groups * epg
    return pl.pallas_call(
        functools.partial(_route_kernel, n_groups=n_groups, epg=epg, exp_row0=exp_row0),
        grid=(n // tm,),
        in_specs=[pl.BlockSpec((tm, LANES), lambda i: (i, 0))],
        out_specs=[
            pl.BlockSpec((8, tm), lambda i: (0, i)),
            pl.BlockSpec((8, tm), lambda i: (0, i)),
            pl.BlockSpec((n_exp, LANES), lambda i: (0, 0)),
        ],
        out_shape=[
            jax.ShapeDtypeStruct((8, n), jnp.int32),
            jax.ShapeDtypeStruct((8, n), F32),
            jax.ShapeDtypeStruct((n_exp, LANES), F32),
        ],
        scratch_shapes=[pltpu.VMEM((n_exp, 1), F32)],
        compiler_params=_params(1),
        name="route",
    )(logits)


def _sorted_row(off_ref, idx_ref, t, k, tm):
    return off_ref[idx_ref[k * tm + t]] + idx_ref[(TOP_K + k) * tm + t]


def _dispatch_kernel(off_ref, idx_ref, h3_ref, xs_ref, sem, *, tm):
    def copy(t, k):
        pos = _sorted_row(off_ref, idx_ref, t, k, tm)
        return pltpu.make_async_copy(h3_ref.at[pl.ds(t * SLAB, SLAB), :],
                                     xs_ref.at[pl.ds(pos * SLAB, SLAB), :], sem)

    def start(t, carry):
        for k in range(TOP_K):
            copy(t, k).start()
        return carry

    def wait(t, carry):
        for k in range(TOP_K):
            copy(t, k).wait()
        return carry

    lax.fori_loop(0, tm, start, 0)
    lax.fori_loop(0, tm, wait, 0)


def _dispatch(offsets, idx_flat, h3, *, tm, n_rows):
    n = h3.shape[0] // SLAB
    grid_spec = pltpu.PrefetchScalarGridSpec(
        num_scalar_prefetch=1,
        grid=(n // tm,),
        in_specs=[
            pl.BlockSpec((2 * TOP_K * tm,), lambda i, off: (i,), memory_space=pltpu.SMEM),
            pl.BlockSpec((tm * SLAB, LANES), lambda i, off: (i, 0)),
        ],
        out_specs=pl.BlockSpec(memory_space=pl.ANY),
        scratch_shapes=[pltpu.SemaphoreType.DMA(())],
    )
    return pl.pallas_call(
        functools.partial(_dispatch_kernel, tm=tm),
        grid_spec=grid_spec,
        out_shape=jax.ShapeDtypeStruct((n_rows * SLAB, LANES), F32),
        compiler_params=_params(1),
        name="dispatch",
    )(offsets, idx_flat, h3)


def _expert_kernel(te_ref, nv_ref, nu_ref, xs_ref, w1_ref, w3_ref, w2_ref, ys_ref, w1b_ref, w3b_ref, w2b_ref):
    i = pl.program_id(0)
    tm = xs_ref.shape[0] // SLAB
    changed = (i == 0) | (te_ref[i] != te_ref[jnp.maximum(i - 1, 0)])

    @pl.when(changed)
    def _():
        w1b_ref[...] = w1_ref[...].astype(BF16)
        w3b_ref[...] = w3_ref[...].astype(BF16)
        w2b_ref[...] = w2_ref[...].astype(BF16)

    nv = nv_ref[i]

    @pl.when(nv > 0)
    def _():
        x = jnp.concatenate([xs_ref[pl.ds(j, tm, stride=SLAB), :] for j in range(SLAB)], axis=1)
        row = lax.broadcasted_iota(jnp.int32, x.shape, 0)
        x = jnp.where(row < nv, x, 0.0).astype(BF16)
        a1 = jnp.dot(x, w1b_ref[...], preferred_element_type=F32)
        a3 = jnp.dot(x, w3b_ref[...], preferred_element_type=F32)
        a = (a1 * (1.0 / (1.0 + jnp.exp(-a1))) * a3).astype(BF16)
        y = jnp.dot(a, w2b_ref[...], preferred_element_type=F32)
        for j in range(SLAB):
            ys_ref[pl.ds(j, tm, stride=SLAB), :] = y[:, j * LANES:(j + 1) * LANES]

    @pl.when(nv == 0)
    def _():
        ys_ref[...] = jnp.zeros(ys_ref.shape, F32)


def _experts(tile_expert, tile_valid, n_used, xs3, w1, w3, w2, *, layer, tm, n_tiles):
    d, de = w1.shape[2], w1.shape[3]
    w_in_spec = pl.BlockSpec((None, None, d, de), lambda i, te, nv, nu: (layer, te[i], 0, 0))
    grid_spec = pltpu.PrefetchScalarGridSpec(
        num_scalar_prefetch=3,
        grid=(n_tiles,),
        in_specs=[
            pl.BlockSpec((tm * SLAB, LANES), lambda i, te, nv, nu: (jnp.minimum(i, nu[0] - 1), 0)),
            w_in_spec, w_in_spec,
            pl.BlockSpec((None, None, de, d), lambda i, te, nv, nu: (layer, te[i], 0, 0)),
        ],
        out_specs=pl.BlockSpec((tm * SLAB, LANES), lambda i, te, nv, nu: (jnp.minimum(i, nu[0]), 0)),
        scratch_shapes=[pltpu.VMEM((d, de), BF16), pltpu.VMEM((d, de), BF16), pltpu.VMEM((de, d), BF16)],
    )
    return pl.pallas_call(
        _expert_kernel,
        grid_spec=grid_spec,
        out_shape=jax.ShapeDtypeStruct(xs3.shape, F32),
        compiler_params=_params(1, 56 * 1024 * 1024),
        name="experts",
    )(tile_expert, tile_valid, n_used, xs3, w1, w3, w2)


def _combine_kernel(off_ref, idx_ref, x1_ref, rw_ref, gfin_ref, ys_ref, o_ref, buf_ref, sem, *, tm):
    def copy(t, k):
        pos = _sorted_row(off_ref, idx_ref, t, k, tm)
        return pltpu.make_async_copy(ys_ref.at[pl.ds(pos * SLAB, SLAB), :],
                                     buf_ref.at[k, pl.ds(t * SLAB, SLAB), :], sem)

    def start(t, carry):
        for k in range(TOP_K):
            copy(t, k).start()
        return carry

    def wait(t, carry):
        for k in range(TOP_K):
            copy(t, k).wait()
        return carry

    lax.fori_loop(0, tm, start, 0)
    lax.fori_loop(0, tm, wait, 0)

    x = x1_ref[...]
    for k in range(TOP_K):
        yk = jnp.concatenate([buf_ref[k, pl.ds(j, tm, stride=SLAB), :] for j in range(SLAB)], axis=1)
        x = x + rw_ref[:, k:k + 1] * yk
    o_ref[...] = _rms(x, gfin_ref[...])


def _combine(offsets, idx_flat, x1, rw_cols, g_final, ys3, *, tm, row_offset, rows):
    d = x1.shape[1]
    off = row_offset // tm
    grid_spec = pltpu.PrefetchScalarGridSpec(
        num_scalar_prefetch=1,
        grid=(rows // tm,),
        in_specs=[
            pl.BlockSpec((2 * TOP_K * tm,), lambda i, o: (i + off,), memory_space=pltpu.SMEM),
            pl.BlockSpec((tm, d), lambda i, o: (i + off, 0)),
            pl.BlockSpec((tm, TOP_K), lambda i, o: (i + off, 0)),
            pl.BlockSpec((1, d), lambda i, o: (0, 0)),
            pl.BlockSpec(memory_space=pl.ANY),
        ],
        out_specs=pl.BlockSpec((tm, d), lambda i, o: (i, 0)),
        scratch_shapes=[pltpu.VMEM((TOP_K, tm * SLAB, LANES), F32), pltpu.SemaphoreType.DMA(())],
    )
    return pl.pallas_call(
        functools.partial(_combine_kernel, tm=tm),
        grid_spec=grid_spec,
        out_shape=jax.ShapeDtypeStruct((rows, d), F32),
        compiler_params=_params(1),
        name="combine",
    )(offsets, idx_flat, x1, rw_cols, g_final, ys3)


def kernel(x_prompt, x_sample, cache_k, cache_v, page_table, g_attn, w_in, lambda_q1, lambda_k1, lambda_q2,
           lambda_k2, g_subln, ln_v_g, ln_v_b, w_spatial, b_spatial, w_out, g_ffn, w_router_grp, b_router_grp,
           w_router_exp, b_router_exp, w1, w3, w2, g_final):
    batch, seq, d = x_prompt.shape
    dec_batch, t, _ = x_sample.shape
    depth, n_phys, page, heads, _, head_dim = cache_k.shape
    groups, chunk = w_spatial.shape[1], w_spatial.shape[2]
    n_groups, n_exp = w_router_grp.shape[2], w_router_exp.shape[2]
    epg = n_exp // n_groups
    n_pages = page_table.shape[1]
    past = n_pages * page
    sec = w_in.shape[2] // 5
    n_p, n_s = batch * seq, dec_batch * t
    n_tok = n_p + n_s
    assert depth == 1 and d == SLAB * LANES and head_dim == LANES
    assert sec == heads * 2 * head_dim == groups * chunk and chunk % t == 0

    l = 0
    lam_init = 0.8 - 0.6 * math.exp(-0.3 * l)
    tm_in, tq, tm_mix, tm_route, tm_row, tm_exp = 512, 256, 256, 512, 256, 256

    xp = x_prompt.reshape(n_p, d)
    xs = x_sample.reshape(n_s, d)
    row = lambda a: a[l].reshape(1, -1)
    lams = [row(a) for a in (lambda_q1, lambda_k1, lambda_q2, lambda_k2)]
    w_in_bf = w_in[l].astype(BF16)
    w_out_bf = w_out[l].astype(BF16)

    cos_p, sin_p = _rope_tables(seq, seq, 0, head_dim)
    cos_s, sin_s = _rope_tables(tm_in, t, past, head_dim)
    proj = functools.partial(_in_projection, g=row(g_attn), w_bf=w_in_bf, ln_g=row(ln_v_g), ln_b=row(ln_v_b),
                             tm=tm_in, head_dim=head_dim)
    q_p, k_p, v_p, u_p, vg_p = proj(xp, cos_t=cos_p, sin_t=sin_p, table_blocks=seq // tm_in, q_dtype=BF16,
                                    name="in_proj_prompt")
    q_s, k_s, v_s, u_s, vg_s = proj(xs, cos_t=cos_s, sin_t=sin_s, table_blocks=1, q_dtype=F32,
                                    name="in_proj_sample")

    att_p = _prompt_attention(lams, row(g_subln), q_p, k_p, v_p, batch=batch, seq=seq, heads=heads,
                              head_dim=head_dim, tq=tq, lam_init=lam_init)
    att_s = _sample_attention(page_table, lams, row(g_subln), q_s, k_s, v_s,
                              cache_k.reshape(depth, n_phys, page, sec), cache_v.reshape(depth, n_phys, page, sec),
                              layer=l, dec_batch=dec_batch, t=t, heads=heads, head_dim=head_dim, lam_init=lam_init)

    exp_row0 = 8
    wr = jnp.zeros((d, LANES), F32).at[:, :n_groups].set(w_router_grp[l])
    wr = wr.at[:, exp_row0:exp_row0 + n_exp].set(w_router_exp[l])
    br = jnp.zeros((1, LANES), F32).at[0, :n_groups].set(b_router_grp[l])
    br = br.at[0, exp_row0:exp_row0 + n_exp].set(b_router_exp[l])
    reps = chunk // t
    wm_s = jnp.tile(w_spatial[l][:, :t, :t], (1, reps, reps))
    bs_s = jnp.tile(b_spatial[l][:, :t], (1, reps))[..., None]
    mix = functools.partial(_mix_out, wo_bf=w_out_bf, g_ffn=row(g_ffn), wr=wr, br=br, n_total=n_tok, chunk=chunk)
    bufs = mix(xp, att_p, u_p, vg_p, w_spatial[l], b_spatial[l][..., None], prev=None, tm=tm_mix, row_offset=0,
               blk=chunk, name="mix_out_prompt")
    x1, h3, logits = mix(xs, att_s, u_s, vg_s, wm_s, bs_s, prev=bufs, tm=tm_mix, row_offset=n_p, blk=t,
                         name="mix_out_sample")

    ri, rw, cnt = _route(logits, tm=tm_route, n_groups=n_groups, epg=epg, exp_row0=exp_row0)
    counts = cnt[:, 0].astype(jnp.int32)
    n_tiles = (n_tok * TOP_K) // tm_exp + n_exp
    tiles_e = (counts + tm_exp - 1) // tm_exp
    tile_end = jnp.cumsum(tiles_e)
    tile_start = tile_end - tiles_e
    n_used = tile_end[-1]
    tile_id = jnp.arange(n_tiles, dtype=jnp.int32)
    tile_e = jnp.minimum(jnp.searchsorted(tile_end, tile_id, side="right"), n_exp - 1).astype(jnp.int32)
    tile_e = jnp.where(tile_id < n_used, tile_e, tile_e[jnp.maximum(n_used - 1, 0)])
    tile_valid = jnp.clip(counts[tile_e] - (tile_id - tile_start[tile_e]) * tm_exp, 0, tm_exp)
    tile_valid = jnp.where(tile_id < n_used, tile_valid, 0).astype(jnp.int32)
    offsets = (tile_start * tm_exp).astype(jnp.int32)
    idx_flat = ri[:2 * TOP_K].reshape(2 * TOP_K, n_tok // tm_row, tm_row).transpose(1, 0, 2).reshape(-1)
    rw_cols = rw[:TOP_K].T

    xs3 = _dispatch(offsets, idx_flat, h3, tm=tm_row, n_rows=n_tiles * tm_exp)
    ys3 = _experts(tile_e, tile_valid, n_used.reshape(1).astype(jnp.int32), xs3, w1, w3, w2, layer=l, tm=tm_exp,
                   n_tiles=n_tiles)
    comb = functools.partial(_combine, offsets, idx_flat, x1, rw_cols, g_final.reshape(1, -1), ys3, tm=tm_row)
    y_p = comb(row_offset=0, rows=n_p)
    y_s = comb(row_offset=n_p, rows=n_s)

    kv6 = lambda a, b_: a.reshape(depth, b_, -1, heads, 2, head_dim)
    v5 = lambda a, b_: a.reshape(depth, b_, -1, heads, 2 * head_dim)
    return (y_p.reshape(batch, seq, d), y_s.reshape(dec_batch, t, d),
            kv6(k_p, batch), v5(v_p, batch), kv6(k_s, dec_batch), v5(v_s, dec_batch),
            vg_p.reshape(batch, seq, -1)[None, :, seq - chunk:], vg_s.reshape(depth, dec_batch, t, -1))
```

```python
import functools
import math

import jax
import jax.numpy as jnp
from jax import lax
from jax.experimental import pallas as pl
from jax.experimental.pallas import tpu as pltpu

F32 = jnp.float32
BF16 = jnp.bfloat16

ROPE_THETA = 10000.0
RMS_EPS = 1e-6
LN_EPS = 1e-5
TOP_K = 2

LANES = 128
SLAB = 16
VMEM_LIMIT = 52 * 1024 * 1024

NT_DIMS = (((1,), (1,)), ((), ()))


def _params(n_axes, vmem=VMEM_LIMIT):
    return pltpu.CompilerParams(dimension_semantics=("arbitrary",) * n_axes, vmem_limit_bytes=vmem)


def _rms(x, g):
    return x * lax.rsqrt(jnp.mean(x * x, axis=-1, keepdims=True) + RMS_EPS) * g


def _rope_table_kernel(cos_ref, sin_ref, *, period, offset, head_dim):
    rows, lanes = cos_ref.shape
    half = head_dim // 2
    row = lax.broadcasted_iota(jnp.int32, (rows, lanes), 0)
    lane = lax.broadcasted_iota(jnp.int32, (rows, lanes), 1)
    pos = (offset + row % period).astype(F32)
    j = (lane % half).astype(F32)
    inv_freq = jnp.exp(-math.log(ROPE_THETA) * j * (2.0 / head_dim))
    ang = pos * inv_freq
    cos_ref[...] = jnp.cos(ang)
    s = jnp.sin(ang)
    sin_ref[...] = jnp.where(lane < half, -s, s)


def _rope_tables(rows, period, offset, head_dim):
    return pl.pallas_call(
        functools.partial(_rope_table_kernel, period=period, offset=offset, head_dim=head_dim),
        out_shape=(jax.ShapeDtypeStruct((rows, head_dim), F32),) * 2,
        name="rope_tables",
    )()


def _in_proj_kernel(x_ref, g_ref, w_ref, cos_ref, sin_ref, lng_ref, lnb_ref,
                    q_ref, k8_ref, kb_ref, v_ref, vb_ref, u_ref, vg_ref, xn_ref, *, q_scale, head_dim):
    j = pl.program_id(1)
    tm = x_ref.shape[0]

    @pl.when(j == 0)
    def _():
        xn_ref[...] = _rms(x_ref[...], g_ref[...]).astype(BF16)

    z = jnp.dot(xn_ref[...], w_ref[...], preferred_element_type=F32)
    n_head_cols = z.shape[1] // head_dim

    def rope(c):
        zc = z[:, c * head_dim:(c + 1) * head_dim]
        return zc * cos_ref[...] + pltpu.roll(zc, head_dim // 2, axis=1) * sin_ref[...]

    def gelu(t):
        return 0.5 * t * (1.0 + lax.erf(t * math.sqrt(0.5)))

    @pl.when(j == 0)
    def _():
        for c in range(n_head_cols):
            q_ref[:, c * head_dim:(c + 1) * head_dim] = rope(c) * q_scale

    @pl.when(j == 1)
    def _():
        for c in range(n_head_cols):
            r = rope(c)
            k8_ref[pl.ds(c, tm, stride=n_head_cols), :] = r
            kb_ref[:, c * head_dim:(c + 1) * head_dim] = r.astype(BF16)

    @pl.when(j == 2)
    def _():
        v_ref[...] = z
        vb_ref[...] = z.astype(BF16)

    @pl.when(j == 3)
    def _():
        u_ref[...] = gelu(z).astype(u_ref.dtype)

    @pl.when(j == 4)
    def _():
        c = gelu(z)
        cc = c - jnp.mean(c, axis=-1, keepdims=True)
        var = jnp.mean(cc * cc, axis=-1, keepdims=True)
        vg_ref[...] = cc * lax.rsqrt(var + LN_EPS) * lng_ref[...] + lnb_ref[...]


def _in_projection(x2d, g, w_bf, cos_t, sin_t, ln_g, ln_b, *, tm, table_blocks, head_dim, name):
    m, d = x2d.shape
    sec = w_bf.shape[1] // 5
    hc = sec // head_dim
    row_blk = lambda i, j: (i, 0)
    fixed = lambda i, j: (0, 0)
    table = lambda i, j: (i % table_blocks, 0)
    wide = lambda dt: (pl.BlockSpec((tm, sec), row_blk), jax.ShapeDtypeStruct((m, sec), dt))
    outs = [wide(F32),
            (pl.BlockSpec((tm * hc, head_dim), row_blk), jax.ShapeDtypeStruct((m * hc, head_dim), F32)),
            wide(BF16), wide(F32), wide(BF16), wide(BF16), wide(F32)]
    return pl.pallas_call(
        functools.partial(_in_proj_kernel, q_scale=head_dim ** -0.5, head_dim=head_dim),
        grid=(m // tm, 5),
        in_specs=[
            pl.BlockSpec((tm, d), row_blk),
            pl.BlockSpec((1, d), fixed),
            pl.BlockSpec((d, sec), lambda i, j: (0, j)),
            pl.BlockSpec((tm, head_dim), table),
            pl.BlockSpec((tm, head_dim), table),
            pl.BlockSpec((1, sec), fixed),
            pl.BlockSpec((1, sec), fixed),
        ],
        out_specs=[o[0] for o in outs],
        out_shape=[o[1] for o in outs],
        scratch_shapes=[pltpu.VMEM((tm, d), BF16)],
        compiler_params=_params(2),
        name=name,
    )(x2d, g, w_bf, cos_t, sin_t, ln_g, ln_b)


def _diff_lambda(lq1_ref, lk1_ref, lq2_ref, lk2_ref, lam_init):
    a = jnp.sum(lq1_ref[...] * lk1_ref[...], axis=-1, keepdims=True)
    b = jnp.sum(lq2_ref[...] * lk2_ref[...], axis=-1, keepdims=True)
    return jnp.exp(a) - jnp.exp(b) + lam_init


def _prompt_attn_kernel(lq1_ref, lk1_ref, lq2_ref, lk2_ref, gsub_ref, q_ref, k_ref, v_ref, o_ref,
                        vt_ref, *, tq, head_dim, lam_init):
    seq = q_ref.shape[0]
    for kb in range(seq // tq):
        vt_ref[:, kb * tq:(kb + 1) * tq] = v_ref[kb * tq:(kb + 1) * tq, :].astype(F32).T.astype(BF16)
    lam = _diff_lambda(lq1_ref, lk1_ref, lq2_ref, lk2_ref, lam_init)
    key = lax.broadcasted_iota(jnp.int32, (tq, tq), 0)
    qry = lax.broadcasted_iota(jnp.int32, (tq, tq), 1)
    causal = key <= qry

    for qi in range(seq // tq):
        n_keys = (qi + 1) * tq
        q_t = q_ref[qi * tq:(qi + 1) * tq, :].T.astype(BF16)
        outs = []
        for c in range(2):
            sl = slice(c * head_dim, (c + 1) * head_dim)
            s = jnp.dot(k_ref[0:n_keys, sl], q_t[sl, :], preferred_element_type=F32)
            diag = jnp.where(causal, s[qi * tq:, :], -jnp.inf)
            s = diag if qi == 0 else jnp.concatenate([s[:qi * tq, :], diag], axis=0)
            p = jnp.exp(s - jnp.max(s, axis=0, keepdims=True))
            denom = jnp.sum(p, axis=0, keepdims=True)
            acc = jnp.dot(vt_ref[:, 0:n_keys], p.astype(BF16), preferred_element_type=F32)
            outs.append(acc / denom)
        o_t = outs[0] - lam * outs[1]
        ms = jnp.mean(o_t * o_t, axis=0, keepdims=True)
        o_t = o_t * lax.rsqrt(ms + RMS_EPS) * gsub_ref[...] * (1.0 - lam_init)
        o_ref[qi * tq:(qi + 1) * tq, :] = o_t.T.astype(o_ref.dtype)


def _prompt_attention(lams, g_sub_col, q2d, k2d, v2d, *, batch, seq, heads, head_dim, tq, lam_init):
    hw = 2 * head_dim
    vec = pl.BlockSpec((1, head_dim), lambda b, h: (0, 0))
    blk = pl.BlockSpec((seq, hw), lambda b, h: (b, h))
    return pl.pallas_call(
        functools.partial(_prompt_attn_kernel, tq=tq, head_dim=head_dim, lam_init=lam_init),
        grid=(batch, heads),
        in_specs=[vec, vec, vec, vec, pl.BlockSpec((hw, 1), lambda b, h: (0, 0)), blk, blk, blk],
        out_specs=blk,
        out_shape=jax.ShapeDtypeStruct(q2d.shape, BF16),
        scratch_shapes=[pltpu.VMEM((hw, seq), BF16)],
        compiler_params=_params(2),
        name="prompt_attention",
    )(*lams, g_sub_col, q2d, k2d, v2d)


def _sample_attn_kernel(pt_ref, lq1_ref, lk1_ref, lq2_ref, lk2_ref, gsub_ref, q_ref, kn_ref, vn_ref, *rest,
                        n_pages, page, heads, head_dim, lam_init):
    kc = rest[:n_pages]
    vc = rest[n_pages:2 * n_pages]
    o_ref, s_ref, knp_ref, vnp_ref = rest[2 * n_pages:]
    t = q_ref.shape[0]
    hw = 2 * head_dim
    n_hc = 2 * heads

    @pl.when(pl.program_id(0) == 0)
    def _():
        knp_ref[...] = jnp.zeros(knp_ref.shape, F32)
        vnp_ref[...] = jnp.zeros(vnp_ref.shape, F32)

    knp_ref[0:t * n_hc, :] = kn_ref[...]
    vnp_ref[0:t, :] = vn_ref[...]

    q = q_ref[...].astype(BF16)
    for j in range(n_pages + 1):
        src = kc[j] if j < n_pages else knp_ref
        for hc in range(n_hc):
            kj = src[pl.ds(hc, page, stride=n_hc), :].astype(BF16)
            s = lax.dot_general(q[:, hc * head_dim:(hc + 1) * head_dim], kj, NT_DIMS,
                                preferred_element_type=F32)
            if j == n_pages:
                row = lax.broadcasted_iota(jnp.int32, s.shape, 0)
                col = lax.broadcasted_iota(jnp.int32, s.shape, 1)
                s = jnp.where(col <= row, s, -jnp.inf)
            s_ref[hc * t:(hc + 1) * t, j * page:(j + 1) * page] = s

    s = s_ref[...]
    p = jnp.exp(s - jnp.max(s, axis=-1, keepdims=True))
    a = p / jnp.sum(p, axis=-1, keepdims=True)
    lam = _diff_lambda(lq1_ref, lk1_ref, lq2_ref, lk2_ref, lam_init)
    for h in range(heads):
        a_h = (a[2 * h * t:(2 * h + 1) * t] - lam * a[(2 * h + 1) * t:(2 * h + 2) * t]).astype(BF16)
        acc = jnp.zeros((t, hw), F32)
        for j in range(n_pages + 1):
            if j < n_pages:
                vj = jnp.concatenate([vc[j][pl.ds(half * heads + h, page, stride=n_hc), :] for half in range(2)],
                                     axis=1).astype(BF16)
            else:
                vj = vnp_ref[:, h * hw:(h + 1) * hw].astype(BF16)
            acc = acc + jnp.dot(a_h[:, j * page:(j + 1) * page], vj, preferred_element_type=F32)
        o_ref[:, h * hw:(h + 1) * hw] = _rms(acc, gsub_ref[...]) * (1.0 - lam_init)


def _sample_attention(page_table, lams, g_sub, q2d, k_rows, v2d, cache_k_rows, cache_v_rows, *, layer, n_phys,
                      page, dec_batch, t, heads, head_dim, lam_init):
    n_pages = page_table.shape[1]
    hw = 2 * head_dim
    n_hc = 2 * heads
    width = q2d.shape[1]
    vec = pl.BlockSpec((1, head_dim), lambda b, pt: (0, 0))
    row_blk = pl.BlockSpec((t, width), lambda b, pt: (b, 0))

    def page_spec(j):
        return pl.BlockSpec((page * n_hc, head_dim), lambda b, pt: (layer * n_phys + pt[b * n_pages + j], 0))

    pages = [page_spec(j) for j in range(n_pages)]
    grid_spec = pltpu.PrefetchScalarGridSpec(
        num_scalar_prefetch=1,
        grid=(dec_batch,),
        in_specs=[vec, vec, vec, vec, pl.BlockSpec((1, hw), lambda b, pt: (0, 0)), row_blk,
                  pl.BlockSpec((t * n_hc, head_dim), lambda b, pt: (b, 0)), row_blk]
        + pages + pages,
        out_specs=row_blk,
        scratch_shapes=[
            pltpu.VMEM((n_hc * t, (n_pages + 1) * page), F32),
            pltpu.VMEM((page * n_hc, head_dim), F32), pltpu.VMEM((page, width), F32),
        ],
    )
    return pl.pallas_call(
        functools.partial(_sample_attn_kernel, n_pages=n_pages, page=page, heads=heads, head_dim=head_dim,
                          lam_init=lam_init),
        grid_spec=grid_spec,
        out_shape=jax.ShapeDtypeStruct(q2d.shape, F32),
        compiler_params=_params(1),
        name="sample_attention",
    )(page_table.reshape(-1), *lams, g_sub, q2d, k_rows, v2d, *([cache_k_rows] * n_pages),
      *([cache_v_rows] * n_pages))


def _mix_out_kernel(x_ref, att_ref, u_ref, vg_ref, wm_ref, bs_ref, wo_ref, gffn_ref, wr_ref, br_ref, *rest,
                    blk, chunk, aliased):
    x1_ref, h3_ref, lg_ref, cm_ref = rest[3:] if aliased else rest
    tm = x_ref.shape[0]
    groups = wm_ref.shape[0]
    da_width = att_ref.shape[1]
    row = lax.broadcasted_iota(jnp.int32, (chunk, chunk), 0)
    col = lax.broadcasted_iota(jnp.int32, (chunk, chunk), 1)
    keep = (row // blk == col // blk) & (col <= row)
    for g in range(groups):
        gs = slice(g * chunk, (g + 1) * chunk)
        wm = jnp.where(keep, wm_ref[g], 0.0).astype(BF16)
        for c in range(tm // chunk):
            rs = slice(c * chunk, (c + 1) * chunk)
            s = jnp.dot(wm, vg_ref[rs, gs].astype(BF16), preferred_element_type=F32) + bs_ref[g]
            cm_ref[rs, gs] = (u_ref[rs, gs].astype(F32) * s).astype(BF16)
    y = jnp.dot(att_ref[...].astype(BF16), wo_ref[0:da_width, :], preferred_element_type=F32)
    y = y + jnp.dot(cm_ref[...], wo_ref[da_width:, :], preferred_element_type=F32)
    x1 = x_ref[...] + y
    x1_ref[...] = x1
    h = _rms(x1, gffn_ref[...])
    for j in range(SLAB):
        h3_ref[pl.ds(j, tm, stride=SLAB), :] = h[:, j * LANES:(j + 1) * LANES]
    h_hi = h.astype(BF16)
    h_lo = (h - h_hi.astype(F32)).astype(BF16)
    lg = jnp.dot(h_hi, wr_ref[0], preferred_element_type=F32)
    lg = lg + jnp.dot(h_lo, wr_ref[0], preferred_element_type=F32)
    lg = lg + jnp.dot(h_hi, wr_ref[1], preferred_element_type=F32)
    lg_ref[...] = lg + br_ref[...]


def _mix_out(x2d, att, u, vg, wm, bs, wo_bf, g_ffn, wr, br, prev, *, tm, row_offset, n_total, blk, chunk, name):
    m, d = x2d.shape
    width = att.shape[1]
    groups = wm.shape[0]
    off = row_offset // tm
    row_blk = lambda w: pl.BlockSpec((tm, w), lambda i: (i, 0))
    fixed2 = lambda a: pl.BlockSpec(a.shape, lambda i: (0, 0))
    fixed3 = lambda a: pl.BlockSpec(a.shape, lambda i: (0, 0, 0))
    in_specs = [row_blk(d), row_blk(width), row_blk(width), row_blk(width), fixed3(wm), fixed3(bs),
                fixed2(wo_bf), fixed2(g_ffn), fixed3(wr), fixed2(br)]
    args = [x2d, att, u, vg, wm, bs, wo_bf, g_ffn, wr, br]
    aliases = {}
    if prev is not None:
        in_specs += [pl.BlockSpec(memory_space=pl.ANY)] * 3
        aliases = {len(args) + n: n for n in range(3)}
        args += list(prev)
    return pl.pallas_call(
        functools.partial(_mix_out_kernel, blk=blk, chunk=chunk, aliased=prev is not None),
        grid=(m // tm,),
        in_specs=in_specs,
        out_specs=[
            pl.BlockSpec((tm, d), lambda i: (i + off, 0)),
            pl.BlockSpec((tm * SLAB, LANES), lambda i: (i + off, 0)),
            pl.BlockSpec((tm, LANES), lambda i: (i + off, 0)),
        ],
        out_shape=[
            jax.ShapeDtypeStruct((n_total, d), F32),
            jax.ShapeDtypeStruct((n_total * SLAB, LANES), F32),
            jax.ShapeDtypeStruct((n_total, LANES), F32),
        ],
        scratch_shapes=[pltpu.VMEM((tm, groups * chunk), BF16)],
        input_output_aliases=aliases,
        compiler_params=_params(1),
        name=name,
    )(*args)


def _route_kernel(lg_ref, ri_ref, rw_ref, cnt_ref, carry_ref, *, n_groups, epg, exp_row0):
    tm = lg_ref.shape[0]
    n_exp = n_groups * epg

    @pl.when(pl.program_id(0) == 0)
    def _():
        carry_ref[...] = jnp.zeros(carry_ref.shape, F32)

    lt = lg_ref[...].T
    gl = lt[0:n_groups]
    gmax = jnp.max(gl, axis=0, keepdims=True)
    sub_g = lax.broadcasted_iota(jnp.int32, gl.shape, 0)
    gidx = jnp.min(jnp.where(gl == gmax, sub_g, n_groups), axis=0, keepdims=True)
    gw = 1.0 / jnp.sum(jnp.exp(gl - gmax), axis=0, keepdims=True)
    esel = jnp.zeros((epg, tm), F32)
    for gi in range(n_groups):
        esel = jnp.where(gidx == gi, lt[exp_row0 + gi * epg:exp_row0 + (gi + 1) * epg], esel)
    sub_e = lax.broadcasted_iota(jnp.int32, esel.shape, 0)
    v1 = jnp.max(esel, axis=0, keepdims=True)
    i1 = jnp.min(jnp.where(esel == v1, sub_e, epg), axis=0, keepdims=True)
    rest = jnp.where(sub_e == i1, -jnp.inf, esel)
    v2 = jnp.max(rest, axis=0, keepdims=True)
    i2 = jnp.min(jnp.where(rest == v2, sub_e, epg), axis=0, keepdims=True)
    tt = jnp.exp(v2 - v1)
    w1 = gw / (1.0 + tt)
    w2 = gw * tt / (1.0 + tt)
    e1 = gidx * epg + i1
    e2 = gidx * epg + i2

    sub_x = lax.broadcasted_iota(jnp.int32, (n_exp, tm), 0)
    oh1 = sub_x == e1
    oh2 = sub_x == e2
    oh = jnp.where(oh1, 1.0, 0.0) + jnp.where(oh2, 1.0, 0.0)
    earlier = lax.broadcasted_iota(jnp.int32, (tm, tm), 0) < lax.broadcasted_iota(jnp.int32, (tm, tm), 1)
    before = jnp.dot(oh.astype(BF16), jnp.where(earlier, 1.0, 0.0).astype(BF16),
                     preferred_element_type=F32) + carry_ref[...]
    r1 = jnp.sum(jnp.where(oh1, before, 0.0), axis=0, keepdims=True)
    r2 = jnp.sum(jnp.where(oh2, before, 0.0), axis=0, keepdims=True)
    carry_ref[...] = carry_ref[...] + jnp.sum(oh, axis=1, keepdims=True)

    zi = jnp.zeros((4, tm), jnp.int32)
    ri_ref[...] = jnp.concatenate([e1, e2, r1.astype(jnp.int32), r2.astype(jnp.int32), zi], axis=0)
    rw_ref[...] = jnp.concatenate([w1, w2, jnp.zeros((6, tm), F32)], axis=0)
    cnt_ref[...] = jnp.broadcast_to(carry_ref[...], cnt_ref.shape)


def _route(logits, *, tm, n_groups, epg, exp_row0):
    n = logits.shape[0]
    n_exp = n_groups * epg
    return pl.pallas_call(
        functools.partial(_route_kernel, n_groups=n_groups, epg=epg, exp_row0=exp_row0),
        grid=(n // tm,),
        in_specs=[pl.BlockSpec((tm, LANES), lambda i: (i, 0))],
        out_specs=[
            pl.BlockSpec((8, tm), lambda i: (0, i)),
            pl.BlockSpec((8, tm), lambda i: (0, i)),
            pl.BlockSpec((n_exp, LANES), lambda i: (0, 0)),
        ],
        out_shape=[
            jax.ShapeDtypeStruct((8, n), jnp.int32),
            jax.ShapeDtypeStruct((8, n), F32),
            jax.ShapeDtypeStruct((n_exp, LANES), F32),
        ],
        scratch_shapes=[pltpu.VMEM((n_exp, 1), F32)],
        compiler_params=_params(1),
        name="route",
    )(logits)


def _sorted_row(off_ref, idx_ref, t, k, tm):
    return off_ref[idx_ref[k * tm + t]] + idx_ref[(TOP_K + k) * tm + t]


def _dispatch_kernel(off_ref, idx_ref, h3_ref, xs_ref, sem, *, tm):
    def copy(t, k):
        pos = _sorted_row(off_ref, idx_ref, t, k, tm)
        return pltpu.make_async_copy(h3_ref.at[pl.ds(t * SLAB, SLAB), :],
                                     xs_ref.at[pl.ds(pos * SLAB, SLAB), :], sem)

    def start(t, carry):
        for k in range(TOP_K):
            copy(t, k).start()
        return carry

    def wait(t, carry):
        for k in range(TOP_K):
            copy(t, k).wait()
        return carry

    lax.fori_loop(0, tm, start, 0)
    lax.fori_loop(0, tm, wait, 0)


def _dispatch(offsets, idx_flat, h3, *, tm, n_rows):
    n = h3.shape[0] // SLAB
    grid_spec = pltpu.PrefetchScalarGridSpec(
        num_scalar_prefetch=1,
        grid=(n // tm,),
        in_specs=[
            pl.BlockSpec((2 * TOP_K * tm,), lambda i, off: (i,), memory_space=pltpu.SMEM),
            pl.BlockSpec((tm * SLAB, LANES), lambda i, off: (i, 0)),
        ],
        out_specs=pl.BlockSpec(memory_space=pl.ANY),
        scratch_shapes=[pltpu.SemaphoreType.DMA(())],
    )
    return pl.pallas_call(
        functools.partial(_dispatch_kernel, tm=tm),
        grid_spec=grid_spec,
        out_shape=jax.ShapeDtypeStruct((n_rows * SLAB, LANES), F32),
        compiler_params=_params(1),
        name="dispatch",
    )(offsets, idx_flat, h3)


def _expert_kernel(te_ref, nv_ref, nu_ref, xs_ref, w1_ref, w3_ref, w2_ref, ys_ref, w1b_ref, w3b_ref, w2b_ref):
    i = pl.program_id(0)
    tm = xs_ref.shape[0] // SLAB
    changed = (i == 0) | (te_ref[i] != te_ref[jnp.maximum(i - 1, 0)])

    @pl.when(changed)
    def _():
        w1b_ref[...] = w1_ref[...].astype(BF16)
        w3b_ref[...] = w3_ref[...].astype(BF16)
        w2b_ref[...] = w2_ref[...].astype(BF16)

    nv = nv_ref[i]

    @pl.when(nv > 0)
    def _():
        x = jnp.concatenate([xs_ref[pl.ds(j, tm, stride=SLAB), :] for j in range(SLAB)], axis=1)
        row = lax.broadcasted_iota(jnp.int32, x.shape, 0)
        x = jnp.where(row < nv, x, 0.0).astype(BF16)
        a1 = jnp.dot(x, w1b_ref[...], preferred_element_type=F32)
        a3 = jnp.dot(x, w3b_ref[...], preferred_element_type=F32)
        a = (a1 * (1.0 / (1.0 + jnp.exp(-a1))) * a3).astype(BF16)
        y = jnp.dot(a, w2b_ref[...], preferred_element_type=F32)
        for j in range(SLAB):
            ys_ref[pl.ds(j, tm, stride=SLAB), :] = y[:, j * LANES:(j + 1) * LANES]

    @pl.when(nv == 0)
    def _():
        ys_ref[...] = jnp.zeros(ys_ref.shape, F32)


def _experts(tile_expert, tile_valid, n_used, xs3, w1, w3, w2, *, layer, tm, n_tiles):
    d, de = w1.shape[2], w1.shape[3]
    w_in_spec = pl.BlockSpec((None, None, d, de), lambda i, te, nv, nu: (layer, te[i], 0, 0))
    grid_spec = pltpu.PrefetchScalarGridSpec(
        num_scalar_prefetch=3,
        grid=(n_tiles,),
        in_specs=[
            pl.BlockSpec((tm * SLAB, LANES), lambda i, te, nv, nu: (jnp.minimum(i, nu[0] - 1), 0)),
            w_in_spec, w_in_spec,
            pl.BlockSpec((None, None, de, d), lambda i, te, nv, nu: (layer, te[i], 0, 0)),
        ],
        out_specs=pl.BlockSpec((tm * SLAB, LANES), lambda i, te, nv, nu: (jnp.minimum(i, nu[0]), 0)),
        scratch_shapes=[pltpu.VMEM((d, de), BF16), pltpu.VMEM((d, de), BF16), pltpu.VMEM((de, d), BF16)],
    )
    return pl.pallas_call(
        _expert_kernel,
        grid_spec=grid_spec,
        out_shape=jax.ShapeDtypeStruct(xs3.shape, F32),
        compiler_params=_params(1, 56 * 1024 * 1024),
        name="experts",
    )(tile_expert, tile_valid, n_used, xs3, w1, w3, w2)


def _combine_kernel(off_ref, idx_ref, x1_ref, rw_ref, gfin_ref, ys_ref, o_ref, buf_ref, sem, *, tm):
    def copy(t, k):
        pos = _sorted_row(off_ref, idx_ref, t, k, tm)
        return pltpu.make_async_copy(ys_ref.at[pl.ds(pos * SLAB, SLAB), :],
                                     buf_ref.at[k, pl.ds(t * SLAB, SLAB), :], sem)

    def start(t, carry):
        for k in range(TOP_K):
            copy(t, k).start()
        return carry

    def wait(t, carry):
        for k in range(TOP_K):
            copy(t, k).wait()
        return carry

    lax.fori_loop(0, tm, start, 0)
    lax.fori_loop(0, tm, wait, 0)

    x = x1_ref[...]
    for k in range(TOP_K):
        yk = jnp.concatenate([buf_ref[k, pl.ds(j, tm, stride=SLAB), :] for j in range(SLAB)], axis=1)
        x = x + rw_ref[:, k:k + 1] * yk
    o_ref[...] = _rms(x, gfin_ref[...])


def _combine(offsets, idx_flat, x1, rw_cols, g_final, ys3, *, tm, row_offset, rows):
    d = x1.shape[1]
    off = row_offset // tm
    grid_spec = pltpu.PrefetchScalarGridSpec(
        num_scalar_prefetch=1,
        grid=(rows // tm,),
        in_specs=[
            pl.BlockSpec((2 * TOP_K * tm,), lambda i, o: (i + off,), memory_space=pltpu.SMEM),
            pl.BlockSpec((tm, d), lambda i, o: (i + off, 0)),
            pl.BlockSpec((tm, TOP_K), lambda i, o: (i + off, 0)),
            pl.BlockSpec((1, d), lambda i, o: (0, 0)),
            pl.BlockSpec(memory_space=pl.ANY),
        ],
        out_specs=pl.BlockSpec((tm, d), lambda i, o: (i, 0)),
        scratch_shapes=[pltpu.VMEM((TOP_K, tm * SLAB, LANES), F32), pltpu.SemaphoreType.DMA(())],
    )
    return pl.pallas_call(
        functools.partial(_combine_kernel, tm=tm),
        grid_spec=grid_spec,
        out_shape=jax.ShapeDtypeStruct((rows, d), F32),
        compiler_params=_params(1),
        name="combine",
    )(offsets, idx_flat, x1, rw_cols, g_final, ys3)


def kernel(x_prompt, x_sample, cache_k, cache_v, page_table, g_attn, w_in, lambda_q1, lambda_k1, lambda_q2,
           lambda_k2, g_subln, ln_v_g, ln_v_b, w_spatial, b_spatial, w_out, g_ffn, w_router_grp, b_router_grp,
           w_router_exp, b_router_exp, w1, w3, w2, g_final):
    batch, seq, d = x_prompt.shape
    dec_batch, t, _ = x_sample.shape
    depth, n_phys, page, heads, _, head_dim = cache_k.shape
    groups, chunk = w_spatial.shape[1], w_spatial.shape[2]
    n_groups, n_exp = w_router_grp.shape[2], w_router_exp.shape[2]
    epg = n_exp // n_groups
    n_pages = page_table.shape[1]
    past = n_pages * page
    sec = w_in.shape[2] // 5
    n_p, n_s = batch * seq, dec_batch * t
    n_tok = n_p + n_s
    assert depth == 1 and d == SLAB * LANES and head_dim == LANES
    assert sec == heads * 2 * head_dim == groups * chunk and chunk % t == 0

    l = 0
    lam_init = 0.8 - 0.6 * math.exp(-0.3 * l)
    tm_in, tq, tm_mix, tm_route, tm_row, tm_exp = 512, 256, 256, 512, 256, 256

    xp = x_prompt.reshape(n_p, d)
    xs = x_sample.reshape(n_s, d)
    row = lambda a: a[l].reshape(1, -1)
    lams = [row(a) for a in (lambda_q1, lambda_k1, lambda_q2, lambda_k2)]
    w_in_bf = w_in[l].astype(BF16)
    w_out_bf = w_out[l].astype(BF16)

    cos_p, sin_p = _rope_tables(seq, seq, 0, head_dim)
    cos_s, sin_s = _rope_tables(tm_in, t, past, head_dim)
    proj = functools.partial(_in_projection, g=row(g_attn), w_bf=w_in_bf, ln_g=row(ln_v_g), ln_b=row(ln_v_b),
                             tm=tm_in, head_dim=head_dim)
    q_p, k8_p, kb_p, v_p, vb_p, u_p, vg_p = proj(xp, cos_t=cos_p, sin_t=sin_p, table_blocks=seq // tm_in,
                                                 name="in_proj_prompt")
    q_s, k8_s, _, v_s, _, u_s, vg_s = proj(xs, cos_t=cos_s, sin_t=sin_s, table_blocks=1, name="in_proj_sample")

    att_p = _prompt_attention(lams, g_subln[l].reshape(-1, 1), q_p, kb_p, vb_p, batch=batch, seq=seq, heads=heads,
                              head_dim=head_dim, tq=tq, lam_init=lam_init)
    cache_k_rows = cache_k.reshape(-1, head_dim)
    cache_v_rows = cache_v.reshape(depth, n_phys, page, heads, 2, head_dim).transpose(0, 1, 2, 4, 3, 5)
    att_s = _sample_attention(page_table, lams, row(g_subln), q_s, k8_s, v_s, cache_k_rows,
                              cache_v_rows.reshape(-1, head_dim), layer=l, n_phys=n_phys, page=page,
                              dec_batch=dec_batch, t=t, heads=heads, head_dim=head_dim, lam_init=lam_init)

    exp_row0 = 8
    wr = jnp.zeros((d, LANES), F32).at[:, :n_groups].set(w_router_grp[l])
    wr = wr.at[:, exp_row0:exp_row0 + n_exp].set(w_router_exp[l])
    wr_hi = wr.astype(BF16)
    wr = jnp.stack([wr_hi, (wr - wr_hi.astype(F32)).astype(BF16)])
    br = jnp.zeros((1, LANES), F32).at[0, :n_groups].set(b_router_grp[l])
    br = br.at[0, exp_row0:exp_row0 + n_exp].set(b_router_exp[l])
    reps = chunk // t
    wm_s = jnp.tile(w_spatial[l][:, :t, :t], (1, reps, reps))
    bs_s = jnp.tile(b_spatial[l][:, :t], (1, reps))[..., None]
    mix = functools.partial(_mix_out, wo_bf=w_out_bf, g_ffn=row(g_ffn), wr=wr, br=br, n_total=n_tok, chunk=chunk)
    bufs = mix(xp, att_p, u_p, vg_p, w_spatial[l], b_spatial[l][..., None], prev=None, tm=tm_mix, row_offset=0,
               blk=chunk, name="mix_out_prompt")
    x1, h3, logits = mix(xs, att_s, u_s, vg_s, wm_s, bs_s, prev=bufs, tm=tm_mix, row_offset=n_p, blk=t,
                         name="mix_out_sample")

    ri, rw, cnt = _route(logits, tm=tm_route, n_groups=n_groups, epg=epg, exp_row0=exp_row0)
    counts = cnt[:, 0].astype(jnp.int32)
    n_tiles = (n_tok * TOP_K) // tm_exp + n_exp
    tiles_e = (counts + tm_exp - 1) // tm_exp
    tile_end = jnp.cumsum(tiles_e)
    tile_start = tile_end - tiles_e
    n_used = tile_end[-1]
    tile_id = jnp.arange(n_tiles, dtype=jnp.int32)
    tile_e = jnp.sum((tile_id[:, None] >= tile_end[None, :]).astype(jnp.int32), axis=1)
    tile_e = jnp.minimum(tile_e, n_exp - 1)
    tile_e = jnp.where(tile_id < n_used, tile_e, tile_e[jnp.maximum(n_used - 1, 0)])
    tile_valid = jnp.clip(counts[tile_e] - (tile_id - tile_start[tile_e]) * tm_exp, 0, tm_exp)
    tile_valid = jnp.where(tile_id < n_used, tile_valid, 0).astype(jnp.int32)
    offsets = (tile_start * tm_exp).astype(jnp.int32)
    idx_flat = ri[:2 * TOP_K].reshape(2 * TOP_K, n_tok // tm_row, tm_row).transpose(1, 0, 2).reshape(-1)
    rw_cols = rw[:TOP_K].T

    xs3 = _dispatch(offsets, idx_flat, h3, tm=tm_row, n_rows=n_tiles * tm_exp)
    ys3 = _experts(tile_e, tile_valid, n_used.reshape(1).astype(jnp.int32), xs3, w1, w3, w2, layer=l, tm=tm_exp,
                   n_tiles=n_tiles)
    comb = functools.partial(_combine, offsets, idx_flat, x1, rw_cols, g_final.reshape(1, -1), ys3, tm=tm_row)
    y_p = comb(row_offset=0, rows=n_p)
    y_s = comb(row_offset=n_p, rows=n_s)

    kv6 = lambda a, b_: a.reshape(depth, b_, -1, heads, 2, head_dim)
    v5 = lambda a, b_: a.reshape(depth, b_, -1, heads, 2 * head_dim)
    return (y_p.reshape(batch, seq, d), y_s.reshape(dec_batch, t, d),
            kv6(k8_p, batch), v5(v_p, batch), kv6(k8_s, dec_batch), v5(v_s, dec_batch),
            vg_p.reshape(batch, seq, -1)[None, :, seq - chunk:], vg_s.reshape(depth, dec_batch, t, -1))
```

```python
import functools
import math

import jax
import jax.numpy as jnp
from jax import lax
from jax.experimental import pallas as pl
from jax.experimental.pallas import tpu as pltpu

F32 = jnp.float32
BF16 = jnp.bfloat16

ROPE_THETA = 10000.0
RMS_EPS = 1e-6
LN_EPS = 1e-5
TOP_K = 2

LANES = 128
SLAB = 16
VMEM_LIMIT = 52 * 1024 * 1024

NT_DIMS = (((1,), (1,)), ((), ()))


def _params(n_axes, vmem=VMEM_LIMIT):
    return pltpu.CompilerParams(dimension_semantics=("arbitrary",) * n_axes, vmem_limit_bytes=vmem)


def _rms(x, g):
    return x * lax.rsqrt(jnp.mean(x * x, axis=-1, keepdims=True) + RMS_EPS) * g


def _rope_table_kernel(cos_ref, sin_ref, *, period, offset, head_dim):
    rows, lanes = cos_ref.shape
    half = head_dim // 2
    row = lax.broadcasted_iota(jnp.int32, (rows, lanes), 0)
    lane = lax.broadcasted_iota(jnp.int32, (rows, lanes), 1)
    pos = (offset + row % period).astype(F32)
    j = (lane % half).astype(F32)
    inv_freq = jnp.exp(-math.log(ROPE_THETA) * j * (2.0 / head_dim))
    ang = pos * inv_freq
    cos_ref[...] = jnp.cos(ang)
    s = jnp.sin(ang)
    sin_ref[...] = jnp.where(lane < half, -s, s)


def _rope_tables(rows, period, offset, head_dim):
    return pl.pallas_call(
        functools.partial(_rope_table_kernel, period=period, offset=offset, head_dim=head_dim),
        out_shape=(jax.ShapeDtypeStruct((rows, head_dim), F32),) * 2,
        name="rope_tables",
    )()


def _in_proj_kernel(x_ref, g_ref, w_ref, cos_ref, sin_ref, lng_ref, lnb_ref,
                    q_ref, k8_ref, kb_ref, v_ref, vb_ref, u_ref, vg_ref, xn_ref, *, q_scale, head_dim):
    j = pl.program_id(1)
    tm = x_ref.shape[0]
    n_head_cols = w_ref.shape[1] // head_dim
    parts = [(p * (tm // 2), tm // 2) for p in range(2)]

    def project(r0, rows):
        return jnp.dot(xn_ref[r0:r0 + rows, :], w_ref[...], preferred_element_type=F32)

    def rope(z, c, r0, rows):
        zc = z[:, c * head_dim:(c + 1) * head_dim]
        return (zc * cos_ref[r0:r0 + rows, :]
                + pltpu.roll(zc, head_dim // 2, axis=1) * sin_ref[r0:r0 + rows, :])

    def gelu(t):
        return 0.5 * t * (1.0 + lax.erf(t * math.sqrt(0.5)))

    @pl.when(j == 0)
    def _():
        for r0, rows in parts:
            xn_ref[r0:r0 + rows, :] = _rms(x_ref[r0:r0 + rows, :], g_ref[...]).astype(BF16)
        for r0, rows in parts:
            z = project(r0, rows)
            for c in range(n_head_cols):
                q_ref[r0:r0 + rows, c * head_dim:(c + 1) * head_dim] = rope(z, c, r0, rows) * q_scale

    @pl.when(j == 1)
    def _():
        for r0, rows in parts:
            z = project(r0, rows)
            for c in range(n_head_cols):
                r = rope(z, c, r0, rows)
                k8_ref[pl.ds(r0 * n_head_cols + c, rows, stride=n_head_cols), :] = r
                kb_ref[r0:r0 + rows, c * head_dim:(c + 1) * head_dim] = r.astype(BF16)

    @pl.when(j == 2)
    def _():
        for r0, rows in parts:
            z = project(r0, rows)
            v_ref[r0:r0 + rows, :] = z
            vb_ref[r0:r0 + rows, :] = z.astype(BF16)

    @pl.when(j == 3)
    def _():
        for r0, rows in parts:
            u_ref[r0:r0 + rows, :] = gelu(project(r0, rows)).astype(u_ref.dtype)

    @pl.when(j == 4)
    def _():
        for r0, rows in parts:
            c = gelu(project(r0, rows))
            cc = c - jnp.mean(c, axis=-1, keepdims=True)
            var = jnp.mean(cc * cc, axis=-1, keepdims=True)
            vg_ref[r0:r0 + rows, :] = cc * lax.rsqrt(var + LN_EPS) * lng_ref[...] + lnb_ref[...]


def _in_projection(x2d, g, w_bf, cos_t, sin_t, ln_g, ln_b, *, tm, table_blocks, head_dim, name):
    m, d = x2d.shape
    sec = w_bf.shape[1] // 5
    hc = sec // head_dim
    row_blk = lambda i, j: (i, 0)
    fixed = lambda i, j: (0, 0)
    table = lambda i, j: (i % table_blocks, 0)
    wide = lambda dt: (pl.BlockSpec((tm, sec), row_blk), jax.ShapeDtypeStruct((m, sec), dt))
    outs = [wide(F32),
            (pl.BlockSpec((tm * hc, head_dim), row_blk), jax.ShapeDtypeStruct((m * hc, head_dim), F32)),
            wide(BF16), wide(F32), wide(BF16), wide(BF16), wide(F32)]
    return pl.pallas_call(
        functools.partial(_in_proj_kernel, q_scale=head_dim ** -0.5, head_dim=head_dim),
        grid=(m // tm, 5),
        in_specs=[
            pl.BlockSpec((tm, d), row_blk),
            pl.BlockSpec((1, d), fixed),
            pl.BlockSpec((d, sec), lambda i, j: (0, j)),
            pl.BlockSpec((tm, head_dim), table),
            pl.BlockSpec((tm, head_dim), table),
            pl.BlockSpec((1, sec), fixed),
            pl.BlockSpec((1, sec), fixed),
        ],
        out_specs=[o[0] for o in outs],
        out_shape=[o[1] for o in outs],
        scratch_shapes=[pltpu.VMEM((tm, d), BF16)],
        compiler_params=_params(2),
        name=name,
    )(x2d, g, w_bf, cos_t, sin_t, ln_g, ln_b)


def _diff_lambda(lq1_ref, lk1_ref, lq2_ref, lk2_ref, lam_init):
    a = jnp.sum(lq1_ref[...] * lk1_ref[...], axis=-1, keepdims=True)
    b = jnp.sum(lq2_ref[...] * lk2_ref[...], axis=-1, keepdims=True)
    return jnp.exp(a) - jnp.exp(b) + lam_init


def _prompt_attn_kernel(lq1_ref, lk1_ref, lq2_ref, lk2_ref, gsub_ref, q_ref, k_ref, v_ref, o_ref,
                        vt_ref, *, tq, head_dim, lam_init):
    seq = q_ref.shape[0]
    for kb in range(seq // tq):
        vt_ref[:, kb * tq:(kb + 1) * tq] = v_ref[kb * tq:(kb + 1) * tq, :].astype(F32).T.astype(BF16)
    lam = _diff_lambda(lq1_ref, lk1_ref, lq2_ref, lk2_ref, lam_init)
    key = lax.broadcasted_iota(jnp.int32, (tq, tq), 0)
    qry = lax.broadcasted_iota(jnp.int32, (tq, tq), 1)
    causal = key <= qry

    for qi in range(seq // tq):
        n_keys = (qi + 1) * tq
        q_t = q_ref[qi * tq:(qi + 1) * tq, :].T.astype(BF16)
        outs = []
        for c in range(2):
            sl = slice(c * head_dim, (c + 1) * head_dim)
            s = jnp.dot(k_ref[0:n_keys, sl], q_t[sl, :], preferred_element_type=F32)
            diag = jnp.where(causal, s[qi * tq:, :], -jnp.inf)
            s = diag if qi == 0 else jnp.concatenate([s[:qi * tq, :], diag], axis=0)
            p = jnp.exp(s - jnp.max(s, axis=0, keepdims=True))
            denom = jnp.sum(p, axis=0, keepdims=True)
            acc = jnp.dot(vt_ref[:, 0:n_keys], p.astype(BF16), preferred_element_type=F32)
            outs.append(acc / denom)
        o_t = outs[0] - lam * outs[1]
        ms = jnp.mean(o_t * o_t, axis=0, keepdims=True)
        o_t = o_t * lax.rsqrt(ms + RMS_EPS) * gsub_ref[...] * (1.0 - lam_init)
        o_ref[qi * tq:(qi + 1) * tq, :] = o_t.T.astype(o_ref.dtype)


def _prompt_attention(lams, g_sub_col, q2d, k2d, v2d, *, batch, seq, heads, head_dim, tq, lam_init):
    hw = 2 * head_dim
    vec = pl.BlockSpec((1, head_dim), lambda b, h: (0, 0))
    blk = pl.BlockSpec((seq, hw), lambda b, h: (b, h))
    return pl.pallas_call(
        functools.partial(_prompt_attn_kernel, tq=tq, head_dim=head_dim, lam_init=lam_init),
        grid=(batch, heads),
        in_specs=[vec, vec, vec, vec, pl.BlockSpec((hw, 1), lambda b, h: (0, 0)), blk, blk, blk],
        out_specs=blk,
        out_shape=jax.ShapeDtypeStruct(q2d.shape, BF16),
        scratch_shapes=[pltpu.VMEM((hw, seq), BF16)],
        compiler_params=_params(2),
        name="prompt_attention",
    )(*lams, g_sub_col, q2d, k2d, v2d)


def _sample_attn_kernel(pt_ref, lq1_ref, lk1_ref, lq2_ref, lk2_ref, gsub_ref, q_ref, kn_ref, vn_ref, *rest,
                        n_pages, page, heads, head_dim, lam_init):
    kc = rest[:n_pages]
    vc = rest[n_pages:2 * n_pages]
    o_ref, s_ref, knp_ref, vnp_ref = rest[2 * n_pages:]
    t = q_ref.shape[0]
    hw = 2 * head_dim
    n_hc = 2 * heads

    @pl.when(pl.program_id(0) == 0)
    def _():
        knp_ref[...] = jnp.zeros(knp_ref.shape, F32)
        vnp_ref[...] = jnp.zeros(vnp_ref.shape, F32)

    knp_ref[0:t * n_hc, :] = kn_ref[...]
    vnp_ref[0:t, :] = vn_ref[...]

    q = q_ref[...].astype(BF16)
    for j in range(n_pages + 1):
        src = kc[j] if j < n_pages else knp_ref
        for hc in range(n_hc):
            kj = src[pl.ds(hc, page, stride=n_hc), :].astype(BF16)
            s = lax.dot_general(q[:, hc * head_dim:(hc + 1) * head_dim], kj, NT_DIMS,
                                preferred_element_type=F32)
            if j == n_pages:
                row = lax.broadcasted_iota(jnp.int32, s.shape, 0)
                col = lax.broadcasted_iota(jnp.int32, s.shape, 1)
                s = jnp.where(col <= row, s, -jnp.inf)
            s_ref[hc * t:(hc + 1) * t, j * page:(j + 1) * page] = s

    s = s_ref[...]
    p = jnp.exp(s - jnp.max(s, axis=-1, keepdims=True))
    a = p / jnp.sum(p, axis=-1, keepdims=True)
    lam = _diff_lambda(lq1_ref, lk1_ref, lq2_ref, lk2_ref, lam_init)
    for h in range(heads):
        a_h = (a[2 * h * t:(2 * h + 1) * t] - lam * a[(2 * h + 1) * t:(2 * h + 2) * t]).astype(BF16)
        acc = jnp.zeros((t, hw), F32)
        for j in range(n_pages + 1):
            if j < n_pages:
                vj = jnp.concatenate([vc[j][pl.ds(half * heads + h, page, stride=n_hc), :] for half in range(2)],
                                     axis=1).astype(BF16)
            else:
                vj = vnp_ref[:, h * hw:(h + 1) * hw].astype(BF16)
            acc = acc + jnp.dot(a_h[:, j * page:(j + 1) * page], vj, preferred_element_type=F32)
        o_ref[:, h * hw:(h + 1) * hw] = _rms(acc, gsub_ref[...]) * (1.0 - lam_init)


def _sample_attention(page_table, lams, g_sub, q2d, k_rows, v2d, cache_k_rows, cache_v_rows, *, layer, n_phys,
                      page, dec_batch, t, heads, head_dim, lam_init):
    n_pages = page_table.shape[1]
    hw = 2 * head_dim
    n_hc = 2 * heads
    width = q2d.shape[1]
    vec = pl.BlockSpec((1, head_dim), lambda b, pt: (0, 0))
    row_blk = pl.BlockSpec((t, width), lambda b, pt: (b, 0))

    def page_spec(j):
        return pl.BlockSpec((page * n_hc, head_dim), lambda b, pt: (layer * n_phys + pt[b * n_pages + j], 0))

    pages = [page_spec(j) for j in range(n_pages)]
    grid_spec = pltpu.PrefetchScalarGridSpec(
        num_scalar_prefetch=1,
        grid=(dec_batch,),
        in_specs=[vec, vec, vec, vec, pl.BlockSpec((1, hw), lambda b, pt: (0, 0)), row_blk,
                  pl.BlockSpec((t * n_hc, head_dim), lambda b, pt: (b, 0)), row_blk]
        + pages + pages,
        out_specs=row_blk,
        scratch_shapes=[
            pltpu.VMEM((n_hc * t, (n_pages + 1) * page), F32),
            pltpu.VMEM((page * n_hc, head_dim), F32), pltpu.VMEM((page, width), F32),
        ],
    )
    return pl.pallas_call(
        functools.partial(_sample_attn_kernel, n_pages=n_pages, page=page, heads=heads, head_dim=head_dim,
                          lam_init=lam_init),
        grid_spec=grid_spec,
        out_shape=jax.ShapeDtypeStruct(q2d.shape, F32),
        compiler_params=_params(1),
        name="sample_attention",
    )(page_table.reshape(-1), *lams, g_sub, q2d, k_rows, v2d, *([cache_k_rows] * n_pages),
      *([cache_v_rows] * n_pages))


def _mix_out_kernel(x_ref, att_ref, u_ref, vg_ref, wm_ref, bs_ref, wo_ref, gffn_ref, wr_ref, br_ref, *rest,
                    blk, chunk, aliased):
    x1_ref, h3_ref, lg_ref, cm_ref = rest[3:] if aliased else rest
    tm = x_ref.shape[0]
    groups = wm_ref.shape[0]
    da_width = att_ref.shape[1]
    row = lax.broadcasted_iota(jnp.int32, (chunk, chunk), 0)
    col = lax.broadcasted_iota(jnp.int32, (chunk, chunk), 1)
    keep = (row // blk == col // blk) & (col <= row)
    for g in range(groups):
        gs = slice(g * chunk, (g + 1) * chunk)
        wm = jnp.where(keep, wm_ref[g], 0.0).astype(BF16)
        for c in range(tm // chunk):
            rs = slice(c * chunk, (c + 1) * chunk)
            s = jnp.dot(wm, vg_ref[rs, gs].astype(BF16), preferred_element_type=F32) + bs_ref[g]
            cm_ref[rs, gs] = (u_ref[rs, gs].astype(F32) * s).astype(BF16)
    y = jnp.dot(att_ref[...].astype(BF16), wo_ref[0:da_width, :], preferred_element_type=F32)
    y = y + jnp.dot(cm_ref[...], wo_ref[da_width:, :], preferred_element_type=F32)
    x1 = x_ref[...] + y
    x1_ref[...] = x1
    h = _rms(x1, gffn_ref[...])
    for j in range(SLAB):
        h3_ref[pl.ds(j, tm, stride=SLAB), :] = h[:, j * LANES:(j + 1) * LANES]
    h_hi = h.astype(BF16)
    h_lo = (h - h_hi.astype(F32)).astype(BF16)
    lg = jnp.dot(h_hi, wr_ref[0], preferred_element_type=F32)
    lg = lg + jnp.dot(h_lo, wr_ref[0], preferred_element_type=F32)
    lg = lg + jnp.dot(h_hi, wr_ref[1], preferred_element_type=F32)
    lg_ref[...] = lg + br_ref[...]


def _mix_out(x2d, att, u, vg, wm, bs, wo_bf, g_ffn, wr, br, prev, *, tm, row_offset, n_total, blk, chunk, name):
    m, d = x2d.shape
    width = att.shape[1]
    groups = wm.shape[0]
    off = row_offset // tm
    row_blk = lambda w: pl.BlockSpec((tm, w), lambda i: (i, 0))
    fixed2 = lambda a: pl.BlockSpec(a.shape, lambda i: (0, 0))
    fixed3 = lambda a: pl.BlockSpec(a.shape, lambda i: (0, 0, 0))
    in_specs = [row_blk(d), row_blk(width), row_blk(width), row_blk(width), fixed3(wm), fixed3(bs),
                fixed2(wo_bf), fixed2(g_ffn), fixed3(wr), fixed2(br)]
    args = [x2d, att, u, vg, wm, bs, wo_bf, g_ffn, wr, br]
    aliases = {}
    if prev is not None:
        in_specs += [pl.BlockSpec(memory_space=pl.ANY)] * 3
        aliases = {len(args) + n: n for n in range(3)}
        args += list(prev)
    return pl.pallas_call(
        functools.partial(_mix_out_kernel, blk=blk, chunk=chunk, aliased=prev is not None),
        grid=(m // tm,),
        in_specs=in_specs,
        out_specs=[
            pl.BlockSpec((tm, d), lambda i: (i + off, 0)),
            pl.BlockSpec((tm * SLAB, LANES), lambda i: (i + off, 0)),
            pl.BlockSpec((tm, LANES), lambda i: (i + off, 0)),
        ],
        out_shape=[
            jax.ShapeDtypeStruct((n_total, d), F32),
            jax.ShapeDtypeStruct((n_total * SLAB, LANES), F32),
            jax.ShapeDtypeStruct((n_total, LANES), F32),
        ],
        scratch_shapes=[pltpu.VMEM((tm, groups * chunk), BF16)],
        input_output_aliases=aliases,
        compiler_params=_params(1),
        name=name,
    )(*args)


def _route_kernel(lg_ref, ri_ref, rw_ref, cnt_ref, carry_ref, *, n_groups, epg, exp_row0):
    tm = lg_ref.shape[0]
    n_exp = n_groups * epg

    @pl.when(pl.program_id(0) == 0)
    def _():
        carry_ref[...] = jnp.zeros(carry_ref.shape, F32)

    lt = lg_ref[...].T
    gl = lt[0:n_groups]
    gmax = jnp.max(gl, axis=0, keepdims=True)
    sub_g = lax.broadcasted_iota(jnp.int32, gl.shape, 0)
    gidx = jnp.min(jnp.where(gl == gmax, sub_g, n_groups), axis=0, keepdims=True)
    gw = 1.0 / jnp.sum(jnp.exp(gl - gmax), axis=0, keepdims=True)
    esel = jnp.zeros((epg, tm), F32)
    for gi in range(n_groups):
        esel = jnp.where(gidx == gi, lt[exp_row0 + gi * epg:exp_row0 + (gi + 1) * epg], esel)
    sub_e = lax.broadcasted_iota(jnp.int32, esel.shape, 0)
    v1 = jnp.max(esel, axis=0, keepdims=True)
    i1 = jnp.min(jnp.where(esel == v1, sub_e, epg), axis=0, keepdims=True)
    rest = jnp.where(sub_e == i1, -jnp.inf, esel)
    v2 = jnp.max(rest, axis=0, keepdims=True)
    i2 = jnp.min(jnp.where(rest == v2, sub_e, epg), axis=0, keepdims=True)
    tt = jnp.exp(v2 - v1)
    w1 = gw / (1.0 + tt)
    w2 = gw * tt / (1.0 + tt)
    e1 = gidx * epg + i1
    e2 = gidx * epg + i2

    sub_x = lax.broadcasted_iota(jnp.int32, (n_exp, tm), 0)
    oh1 = sub_x == e1
    oh2 = sub_x == e2
    oh = jnp.where(oh1, 1.0, 0.0) + jnp.where(oh2, 1.0, 0.0)
    earlier = lax.broadcasted_iota(jnp.int32, (tm, tm), 0) < lax.broadcasted_iota(jnp.int32, (tm, tm), 1)
    before = jnp.dot(oh.astype(BF16), jnp.where(earlier, 1.0, 0.0).astype(BF16),
                     preferred_element_type=F32) + carry_ref[...]
    r1 = jnp.sum(jnp.where(oh1, before, 0.0), axis=0, keepdims=True)
    r2 = jnp.sum(jnp.where(oh2, before, 0.0), axis=0, keepdims=True)
    carry_ref[...] = carry_ref[...] + jnp.sum(oh, axis=1, keepdims=True)

    zi = jnp.zeros((4, tm), jnp.int32)
    ri_ref[...] = jnp.concatenate([e1, e2, r1.astype(jnp.int32), r2.astype(jnp.int32), zi], axis=0)
    rw_ref[...] = jnp.concatenate([w1, w2, jnp.zeros((6, tm), F32)], axis=0)
    cnt_ref[...] = jnp.broadcast_to(carry_ref[...], cnt_ref.shape)


def _route(logits, *, tm, n_groups, epg, exp_row0):
    n = logits.shape[0]
    n_exp = n_groups * epg
    return pl.pallas_call(
        functools.partial(_route_kernel, n_groups=n_groups, epg=epg, exp_row0=exp_row0),
        grid=(n // tm,),
        in_specs=[pl.BlockSpec((tm, LANES), lambda i: (i, 0))],
        out_specs=[
            pl.BlockSpec((8, tm), lambda i: (0, i)),
            pl.BlockSpec((8, tm), lambda i: (0, i)),
            pl.BlockSpec((n_exp, LANES), lambda i: (0, 0)),
        ],
        out_shape=[
            jax.ShapeDtypeStruct((8, n), jnp.int32),
            jax.ShapeDtypeStruct((8, n), F32),
            jax.ShapeDtypeStruct((n_exp, LANES), F32),
        ],
        scratch_shapes=[pltpu.VMEM((n_exp, 1), F32)],
        compiler_params=_params(1),
        name="route",
    )(logits)


ISSUE_UNROLL = 8


def _dispatch_kernel(pos_ref, h3_ref, xs_ref, sem, *, tm):
    def start(t, carry):
        for k in range(TOP_K):
            pos = pos_ref[k * tm + t]
            pltpu.make_async_copy(h3_ref.at[pl.ds(t * SLAB, SLAB), :],
                                  xs_ref.at[pl.ds(pos * SLAB, SLAB), :], sem).start()
        return carry

    lax.fori_loop(0, tm, start, 0, unroll=ISSUE_UNROLL)
    for k in range(TOP_K):
        pltpu.make_async_copy(h3_ref, xs_ref.at[pl.ds(0, tm * SLAB), :], sem).wait()


def _dispatch(pos_flat, h3, *, tm, n_rows):
    n = h3.shape[0] // SLAB
    return pl.pallas_call(
        functools.partial(_dispatch_kernel, tm=tm),
        grid=(n // tm,),
        in_specs=[
            pl.BlockSpec((TOP_K * tm,), lambda i: (i,), memory_space=pltpu.SMEM),
            pl.BlockSpec((tm * SLAB, LANES), lambda i: (i, 0)),
        ],
        out_specs=pl.BlockSpec(memory_space=pl.ANY),
        out_shape=jax.ShapeDtypeStruct((n_rows * SLAB, LANES), F32),
        scratch_shapes=[pltpu.SemaphoreType.DMA(())],
        compiler_params=_params(1),
        name="dispatch",
    )(pos_flat, h3)


def _expert_kernel(te_ref, nv_ref, nu_ref, xs_ref, w1_ref, w3_ref, w2_ref, ys_ref, w1b_ref, w3b_ref, w2b_ref):
    i = pl.program_id(0)
    tm = xs_ref.shape[0] // SLAB
    changed = (i == 0) | (te_ref[i] != te_ref[jnp.maximum(i - 1, 0)])

    @pl.when(changed)
    def _():
        w1b_ref[...] = w1_ref[...].astype(BF16)
        w3b_ref[...] = w3_ref[...].astype(BF16)
        w2b_ref[...] = w2_ref[...].astype(BF16)

    nv = nv_ref[i]
    half = tm // 2

    def swiglu_rows(r0, rows):
        x = jnp.concatenate([xs_ref[pl.ds(r0 * SLAB + j, rows, stride=SLAB), :] for j in range(SLAB)], axis=1)
        row = r0 + lax.broadcasted_iota(jnp.int32, x.shape, 0)
        x = jnp.where(row < nv, x, 0.0).astype(BF16)
        a1 = jnp.dot(x, w1b_ref[...], preferred_element_type=F32)
        a3 = jnp.dot(x, w3b_ref[...], preferred_element_type=F32)
        a = (a1 * (1.0 / (1.0 + jnp.exp(-a1))) * a3).astype(BF16)
        y = jnp.dot(a, w2b_ref[...], preferred_element_type=F32)
        for j in range(SLAB):
            ys_ref[pl.ds(r0 * SLAB + j, rows, stride=SLAB), :] = y[:, j * LANES:(j + 1) * LANES]

    def zero_rows(r0, rows):
        ys_ref[r0 * SLAB:(r0 + rows) * SLAB, :] = jnp.zeros((rows * SLAB, LANES), F32)

    @pl.when(nv > half)
    def _():
        swiglu_rows(0, half)
        swiglu_rows(half, half)

    @pl.when((nv > 0) & (nv <= half))
    def _():
        swiglu_rows(0, half)
        zero_rows(half, half)

    @pl.when(nv == 0)
    def _():
        zero_rows(0, tm)


def _experts(tile_expert, tile_valid, n_used, xs3, w1, w3, w2, *, layer, tm, n_tiles):
    d, de = w1.shape[2], w1.shape[3]
    w_in_spec = pl.BlockSpec((None, None, d, de), lambda i, te, nv, nu: (layer, te[i], 0, 0))
    grid_spec = pltpu.PrefetchScalarGridSpec(
        num_scalar_prefetch=3,
        grid=(n_tiles,),
        in_specs=[
            pl.BlockSpec((tm * SLAB, LANES), lambda i, te, nv, nu: (jnp.minimum(i, nu[0] - 1), 0)),
            w_in_spec, w_in_spec,
            pl.BlockSpec((None, None, de, d), lambda i, te, nv, nu: (layer, te[i], 0, 0)),
        ],
        out_specs=pl.BlockSpec((tm * SLAB, LANES), lambda i, te, nv, nu: (jnp.minimum(i, nu[0]), 0)),
        scratch_shapes=[pltpu.VMEM((d, de), BF16), pltpu.VMEM((d, de), BF16), pltpu.VMEM((de, d), BF16)],
    )
    return pl.pallas_call(
        _expert_kernel,
        grid_spec=grid_spec,
        out_shape=jax.ShapeDtypeStruct(xs3.shape, F32),
        compiler_params=_params(1, 56 * 1024 * 1024),
        name="experts",
    )(tile_expert, tile_valid, n_used, xs3, w1, w3, w2)


def _combine_kernel(pos_ref, pos_next_ref, x1_ref, rw_ref, gfin_ref, ys_ref, o_ref, buf_ref, sem, *, tm):
    i = pl.program_id(0)
    slot = i % 2

    def gather(p_ref, s):
        def start(t, carry):
            for k in range(TOP_K):
                pos = p_ref[k * tm + t]
                pltpu.make_async_copy(ys_ref.at[pl.ds(pos * SLAB, SLAB), :],
                                      buf_ref.at[s, k, pl.ds(t * SLAB, SLAB), :], sem.at[s]).start()
            return carry
        lax.fori_loop(0, tm, start, 0, unroll=ISSUE_UNROLL)

    @pl.when(i == 0)
    def _():
        gather(pos_ref, 0)

    @pl.when(i + 1 < pl.num_programs(0))
    def _():
        gather(pos_next_ref, 1 - slot)

    for k in range(TOP_K):
        pltpu.make_async_copy(ys_ref.at[pl.ds(0, tm * SLAB), :], buf_ref.at[slot, k], sem.at[slot]).wait()

    x = x1_ref[...]
    for k in range(TOP_K):
        yk = jnp.concatenate([buf_ref[slot, k, pl.ds(j, tm, stride=SLAB), :] for j in range(SLAB)], axis=1)
        x = x + rw_ref[:, k:k + 1] * yk
    o_ref[...] = _rms(x, gfin_ref[...])


def _combine(pos_flat, x1, rw_cols, g_final, ys3, *, tm, row_offset, rows):
    d = x1.shape[1]
    off = row_offset // tm
    last = off + rows // tm - 1
    return pl.pallas_call(
        functools.partial(_combine_kernel, tm=tm),
        grid=(rows // tm,),
        in_specs=[
            pl.BlockSpec((TOP_K * tm,), lambda i: (i + off,), memory_space=pltpu.SMEM),
            pl.BlockSpec((TOP_K * tm,), lambda i: (jnp.minimum(i + off + 1, last),), memory_space=pltpu.SMEM),
            pl.BlockSpec((tm, d), lambda i: (i + off, 0)),
            pl.BlockSpec((tm, TOP_K), lambda i: (i + off, 0)),
            pl.BlockSpec((1, d), lambda i: (0, 0)),
            pl.BlockSpec(memory_space=pl.ANY),
        ],
        out_specs=pl.BlockSpec((tm, d), lambda i: (i, 0)),
        out_shape=jax.ShapeDtypeStruct((rows, d), F32),
        scratch_shapes=[pltpu.VMEM((2, TOP_K, tm * SLAB, LANES), F32), pltpu.SemaphoreType.DMA((2,))],
        compiler_params=_params(1),
        name="combine",
    )(pos_flat, pos_flat, x1, rw_cols, g_final, ys3)


def _moe(x1, h3, logits, w1, w3, w2, g_final, *, layer, n_groups, epg, exp_row0, tm_route, tm_row, tm_exp):
    n_tok = x1.shape[0]
    n_exp = n_groups * epg
    ri, rw, cnt = _route(logits, tm=tm_route, n_groups=n_groups, epg=epg, exp_row0=exp_row0)
    counts = cnt[:, 0].astype(jnp.int32)
    n_tiles = (n_tok * TOP_K) // tm_exp + n_exp
    tiles_e = (counts + tm_exp - 1) // tm_exp
    tile_end = jnp.cumsum(tiles_e)
    tile_start = tile_end - tiles_e
    n_used = tile_end[-1]
    tile_id = jnp.arange(n_tiles, dtype=jnp.int32)
    tile_e = jnp.sum((tile_id[:, None] >= tile_end[None, :]).astype(jnp.int32), axis=1)
    tile_e = jnp.minimum(tile_e, n_exp - 1)
    tile_e = jnp.where(tile_id < n_used, tile_e, tile_e[jnp.maximum(n_used - 1, 0)])
    tile_valid = jnp.clip(counts[tile_e] - (tile_id - tile_start[tile_e]) * tm_exp, 0, tm_exp)
    tile_valid = jnp.where(tile_id < n_used, tile_valid, 0).astype(jnp.int32)
    offsets = (tile_start * tm_exp).astype(jnp.int32)
    pos = offsets[ri[:TOP_K]] + ri[TOP_K:2 * TOP_K]
    pos_flat = pos.reshape(TOP_K, n_tok // tm_row, tm_row).transpose(1, 0, 2).reshape(-1)
    rw_cols = rw[:TOP_K].T

    xs3 = _dispatch(pos_flat, h3, tm=tm_row, n_rows=n_tiles * tm_exp)
    ys3 = _experts(tile_e, tile_valid, n_used.reshape(1).astype(jnp.int32), xs3, w1, w3, w2, layer=layer,
                   tm=tm_exp, n_tiles=n_tiles)
    return functools.partial(_combine, pos_flat, x1, rw_cols, g_final.reshape(1, -1), ys3, tm=tm_row)


def kernel(x_prompt, x_sample, cache_k, cache_v, page_table, g_attn, w_in, lambda_q1, lambda_k1, lambda_q2,
           lambda_k2, g_subln, ln_v_g, ln_v_b, w_spatial, b_spatial, w_out, g_ffn, w_router_grp, b_router_grp,
           w_router_exp, b_router_exp, w1, w3, w2, g_final):
    batch, seq, d = x_prompt.shape
    dec_batch, t, _ = x_sample.shape
    depth, n_phys, page, heads, _, head_dim = cache_k.shape
    groups, chunk = w_spatial.shape[1], w_spatial.shape[2]
    n_groups, n_exp = w_router_grp.shape[2], w_router_exp.shape[2]
    epg = n_exp // n_groups
    n_pages = page_table.shape[1]
    past = n_pages * page
    sec = w_in.shape[2] // 5
    n_p, n_s = batch * seq, dec_batch * t
    n_tok = n_p + n_s
    assert depth == 1 and d == SLAB * LANES and head_dim == LANES
    assert sec == heads * 2 * head_dim == groups * chunk and chunk % t == 0

    l = 0
    lam_init = 0.8 - 0.6 * math.exp(-0.3 * l)
    tm_in, tq, tm_mix, tm_route, tm_row, tm_exp = 512, 256, 256, 512, 512, 256

    xp = x_prompt.reshape(n_p, d)
    xs = x_sample.reshape(n_s, d)
    row = lambda a: a[l].reshape(1, -1)
    lams = [row(a) for a in (lambda_q1, lambda_k1, lambda_q2, lambda_k2)]
    w_in_bf = w_in[l].astype(BF16)
    w_out_bf = w_out[l].astype(BF16)

    cos_p, sin_p = _rope_tables(seq, seq, 0, head_dim)
    cos_s, sin_s = _rope_tables(tm_in, t, past, head_dim)
    proj = functools.partial(_in_projection, g=row(g_attn), w_bf=w_in_bf, ln_g=row(ln_v_g), ln_b=row(ln_v_b),
                             tm=tm_in, head_dim=head_dim)
    q_p, k8_p, kb_p, v_p, vb_p, u_p, vg_p = proj(xp, cos_t=cos_p, sin_t=sin_p, table_blocks=seq // tm_in,
                                                 name="in_proj_prompt")
    q_s, k8_s, _, v_s, _, u_s, vg_s = proj(xs, cos_t=cos_s, sin_t=sin_s, table_blocks=1, name="in_proj_sample")

    att_p = _prompt_attention(lams, g_subln[l].reshape(-1, 1), q_p, kb_p, vb_p, batch=batch, seq=seq, heads=heads,
                              head_dim=head_dim, tq=tq, lam_init=lam_init)
    cache_k_rows = cache_k.reshape(-1, head_dim)
    cache_v_rows = cache_v.reshape(depth, n_phys, page, heads, 2, head_dim).transpose(0, 1, 2, 4, 3, 5)
    att_s = _sample_attention(page_table, lams, row(g_subln), q_s, k8_s, v_s, cache_k_rows,
                              cache_v_rows.reshape(-1, head_dim), layer=l, n_phys=n_phys, page=page,
                              dec_batch=dec_batch, t=t, heads=heads, head_dim=head_dim, lam_init=lam_init)

    exp_row0 = 8
    wr = jnp.zeros((d, LANES), F32).at[:, :n_groups].set(w_router_grp[l])
    wr = wr.at[:, exp_row0:exp_row0 + n_exp].set(w_router_exp[l])
    wr_hi = wr.astype(BF16)
    wr = jnp.stack([wr_hi, (wr - wr_hi.astype(F32)).astype(BF16)])
    br = jnp.zeros((1, LANES), F32).at[0, :n_groups].set(b_router_grp[l])
    br = br.at[0, exp_row0:exp_row0 + n_exp].set(b_router_exp[l])
    reps = chunk // t
    wm_s = jnp.tile(w_spatial[l][:, :t, :t], (1, reps, reps))
    bs_s = jnp.tile(b_spatial[l][:, :t], (1, reps))[..., None]
    mix = functools.partial(_mix_out, wo_bf=w_out_bf, g_ffn=row(g_ffn), wr=wr, br=br, n_total=n_tok, chunk=chunk)
    bufs = mix(xp, att_p, u_p, vg_p, w_spatial[l], b_spatial[l][..., None], prev=None, tm=tm_mix, row_offset=0,
               blk=chunk, name="mix_out_prompt")
    x1, h3, logits = mix(xs, att_s, u_s, vg_s, wm_s, bs_s, prev=bufs, tm=tm_mix, row_offset=n_p, blk=t,
                         name="mix_out_sample")

    comb = _moe(x1, h3, logits, w1, w3, w2, g_final, layer=l, n_groups=n_groups, epg=epg, exp_row0=exp_row0,
                tm_route=tm_route, tm_row=tm_row, tm_exp=tm_exp)
    y_p = comb(row_offset=0, rows=n_p)
    y_s = comb(row_offset=n_p, rows=n_s)

    kv6 = lambda a, b_: a.reshape(depth, b_, -1, heads, 2, head_dim)
    v5 = lambda a, b_: a.reshape(depth, b_, -1, heads, 2 * head_dim)
    return (y_p.reshape(batch, seq, d), y_s.reshape(dec_batch, t, d),
            kv6(k8_p, batch), v5(v_p, batch), kv6(k8_s, dec_batch), v5(v_s, dec_batch),
            vg_p.reshape(batch, seq, -1)[None, :, seq - chunk:], vg_s.reshape(depth, dec_batch, t, -1))
```

```python
import functools
import math

import jax
import jax.numpy as jnp
from jax import lax
from jax.experimental import pallas as pl
from jax.experimental.pallas import tpu as pltpu

F32 = jnp.float32
BF16 = jnp.bfloat16

ROPE_THETA = 10000.0
RMS_EPS = 1e-6
LN_EPS = 1e-5
TOP_K = 2

LANES = 128
VMEM_LIMIT = 52 * 1024 * 1024

NT_DIMS = (((1,), (1,)), ((), ()))


def _params(n_axes, vmem=VMEM_LIMIT):
    return pltpu.CompilerParams(dimension_semantics=("arbitrary",) * n_axes, vmem_limit_bytes=vmem)


def _rms(x, g):
    return x * lax.rsqrt(jnp.mean(x * x, axis=-1, keepdims=True) + RMS_EPS) * g


def _rope_table_kernel(cos_ref, sin_ref, *, period, offset, head_dim):
    rows, lanes = cos_ref.shape
    half = head_dim // 2
    row = lax.broadcasted_iota(jnp.int32, (rows, lanes), 0)
    lane = lax.broadcasted_iota(jnp.int32, (rows, lanes), 1)
    pos = (offset + row % period).astype(F32)
    j = (lane % half).astype(F32)
    inv_freq = jnp.exp(-math.log(ROPE_THETA) * j * (2.0 / head_dim))
    ang = pos * inv_freq
    cos_ref[...] = jnp.cos(ang)
    s = jnp.sin(ang)
    sin_ref[...] = jnp.where(lane < half, -s, s)


def _rope_tables(rows, period, offset, head_dim):
    return pl.pallas_call(
        functools.partial(_rope_table_kernel, period=period, offset=offset, head_dim=head_dim),
        out_shape=(jax.ShapeDtypeStruct((rows, head_dim), F32),) * 2,
        name="rope_tables",
    )()


def _in_proj_kernel(x_ref, g_ref, w_ref, cos_ref, sin_ref, lng_ref, lnb_ref,
                    q_ref, k8_ref, kb_ref, v_ref, vb_ref, u_ref, vg_ref, xn_ref, *, q_scale, head_dim):
    j = pl.program_id(1)
    tm = x_ref.shape[0]
    n_head_cols = w_ref.shape[1] // head_dim
    parts = [(p * (tm // 2), tm // 2) for p in range(2)]

    def project(r0, rows):
        return jnp.dot(xn_ref[r0:r0 + rows, :], w_ref[...], preferred_element_type=F32)

    def rope(z, c, r0, rows):
        zc = z[:, c * head_dim:(c + 1) * head_dim]
        return (zc * cos_ref[r0:r0 + rows, :]
                + pltpu.roll(zc, head_dim // 2, axis=1) * sin_ref[r0:r0 + rows, :])

    def gelu(t):
        return 0.5 * t * (1.0 + lax.erf(t * math.sqrt(0.5)))

    @pl.when(j == 0)
    def _():
        for r0, rows in parts:
            xn_ref[r0:r0 + rows, :] = _rms(x_ref[r0:r0 + rows, :], g_ref[...]).astype(BF16)
        for r0, rows in parts:
            z = project(r0, rows)
            for c in range(n_head_cols):
                q_ref[r0:r0 + rows, c * head_dim:(c + 1) * head_dim] = rope(z, c, r0, rows) * q_scale

    @pl.when(j == 1)
    def _():
        for r0, rows in parts:
            z = project(r0, rows)
            for c in range(n_head_cols):
                r = rope(z, c, r0, rows)
                k8_ref[pl.ds(r0 * n_head_cols + c, rows, stride=n_head_cols), :] = r
                kb_ref[r0:r0 + rows, c * head_dim:(c + 1) * head_dim] = r.astype(BF16)

    @pl.when(j == 2)
    def _():
        for r0, rows in parts:
            z = project(r0, rows)
            v_ref[r0:r0 + rows, :] = z
            vb_ref[r0:r0 + rows, :] = z.astype(BF16)

    @pl.when(j == 3)
    def _():
        for r0, rows in parts:
            u_ref[r0:r0 + rows, :] = gelu(project(r0, rows)).astype(u_ref.dtype)

    @pl.when(j == 4)
    def _():
        for r0, rows in parts:
            c = gelu(project(r0, rows))
            cc = c - jnp.mean(c, axis=-1, keepdims=True)
            var = jnp.mean(cc * cc, axis=-1, keepdims=True)
            vg_ref[r0:r0 + rows, :] = cc * lax.rsqrt(var + LN_EPS) * lng_ref[...] + lnb_ref[...]


def _in_projection(x2d, g, w_bf, cos_t, sin_t, ln_g, ln_b, *, tm, table_blocks, head_dim, name):
    m, d = x2d.shape
    sec = w_bf.shape[1] // 5
    hc = sec // head_dim
    row_blk = lambda i, j: (i, 0)
    fixed = lambda i, j: (0, 0)
    table = lambda i, j: (i % table_blocks, 0)
    wide = lambda dt: (pl.BlockSpec((tm, sec), row_blk), jax.ShapeDtypeStruct((m, sec), dt))
    outs = [wide(F32),
            (pl.BlockSpec((tm * hc, head_dim), row_blk), jax.ShapeDtypeStruct((m * hc, head_dim), F32)),
            wide(BF16), wide(F32), wide(BF16), wide(BF16), wide(F32)]
    return pl.pallas_call(
        functools.partial(_in_proj_kernel, q_scale=head_dim ** -0.5, head_dim=head_dim),
        grid=(m // tm, 5),
        in_specs=[
            pl.BlockSpec((tm, d), row_blk),
            pl.BlockSpec((1, d), fixed),
            pl.BlockSpec((d, sec), lambda i, j: (0, j)),
            pl.BlockSpec((tm, head_dim), table),
            pl.BlockSpec((tm, head_dim), table),
            pl.BlockSpec((1, sec), fixed),
            pl.BlockSpec((1, sec), fixed),
        ],
        out_specs=[o[0] for o in outs],
        out_shape=[o[1] for o in outs],
        scratch_shapes=[pltpu.VMEM((tm, d), BF16)],
        compiler_params=_params(2),
        name=name,
    )(x2d, g, w_bf, cos_t, sin_t, ln_g, ln_b)


def _diff_lambda(lq1_ref, lk1_ref, lq2_ref, lk2_ref, lam_init):
    a = jnp.sum(lq1_ref[...] * lk1_ref[...], axis=-1, keepdims=True)
    b = jnp.sum(lq2_ref[...] * lk2_ref[...], axis=-1, keepdims=True)
    return jnp.exp(a) - jnp.exp(b) + lam_init


def _prompt_attn_kernel(lq1_ref, lk1_ref, lq2_ref, lk2_ref, gsub_ref, q_ref, k_ref, v_ref, o_ref,
                        vt_ref, *, tq, head_dim, lam_init):
    seq = q_ref.shape[0]
    for kb in range(seq // tq):
        vt_ref[:, kb * tq:(kb + 1) * tq] = v_ref[kb * tq:(kb + 1) * tq, :].astype(F32).T.astype(BF16)
    lam =_diff_lambda(lq1_ref, lk1_ref, lq2_ref, lk2_ref, lam_init)
    key = lax.broadcasted_iota(jnp.int32, (tq, tq), 0)
    qry = lax.broadcasted_iota(jnp.int32, (tq, tq), 1)
    causal = key <= qry

    for qi in range(seq // tq):
        n_keys = (qi + 1) * tq
        q_t = q_ref[qi * tq:(qi + 1) * tq, :].T.astype(BF16)
        outs = []
        for c in range(2):
            sl = slice(c * head_dim, (c + 1) * head_dim)
            s = jnp.dot(k_ref[0:n_keys, sl], q_t[sl, :], preferred_element_type=F32)
            diag = jnp.where(causal, s[qi * tq:, :], -jnp.inf)
            s = diag if qi == 0 else jnp.concatenate([s[:qi * tq, :], diag], axis=0)
            p = jnp.exp(s - jnp.max(s, axis=0, keepdims=True))
            denom = jnp.sum(p, axis=0, keepdims=True)
            acc = jnp.dot(vt_ref[:, 0:n_keys], p.astype(BF16), preferred_element_type=F32)
            outs.append(acc / denom)
        o_t = outs[0] - lam * outs[1]
        ms = jnp.mean(o_t * o_t, axis=0, keepdims=True)
        o_t = o_t * lax.rsqrt(ms + RMS_EPS) * gsub_ref[...] * (1.0 - lam_init)
        o_ref[qi * tq:(qi + 1) * tq, :] = o_t.T.astype(o_ref.dtype)


def _prompt_attention(lams, g_sub_col, q2d, k2d, v2d, *, batch, seq, heads, head_dim, tq, lam_init):
    hw = 2 * head_dim
    vec = pl.BlockSpec((1, head_dim), lambda b, h: (0, 0))
    blk = pl.BlockSpec((seq, hw), lambda b, h: (b, h))
    return pl.pallas_call(
        functools.partial(_prompt_attn_kernel, tq=tq, head_dim=head_dim, lam_init=lam_init),
        grid=(batch, heads),
        in_specs=[vec, vec, vec, vec, pl.BlockSpec((hw, 1), lambda b, h: (0, 0)), blk, blk, blk],
        out_specs=blk,
        out_shape=jax.ShapeDtypeStruct(q2d.shape, BF16),
        scratch_shapes=[pltpu.VMEM((hw, seq), BF16)],
        compiler_params=_params(2),
        name="prompt_attention",
    )(*lams, g_sub_col, q2d, k2d, v2d)


def _sample_attn_kernel(pt_ref, lq1_ref, lk1_ref, lq2_ref, lk2_ref, gsub_ref, q_ref, kn_ref, vn_ref, *rest,
                        n_pages, page, heads, head_dim, lam_init):
    kc = rest[:n_pages]
    vc = rest[n_pages:2 * n_pages]
    o_ref, s_ref, knp_ref, vnp_ref = rest[2 * n_pages:]
    t = q_ref.shape[0]
    hw = 2 * head_dim
    n_hc = 2 * heads

    @pl.when(pl.program_id(0) == 0)
    def _():
        knp_ref[...] = jnp.zeros(knp_ref.shape, F32)
        vnp_ref[...] = jnp.zeros(vnp_ref.shape, F32)

    knp_ref[0:t * n_hc, :] = kn_ref[...]
    vnp_ref[0:t, :] = vn_ref[...]

    q = q_ref[...].astype(BF16)
    for j in range(n_pages + 1):
        src = kc[j] if j < n_pages else knp_ref
        for hc in range(n_hc):
            kj = src[pl.ds(hc, page, stride=n_hc), :].astype(BF16)
            s = lax.dot_general(q[:, hc * head_dim:(hc + 1) * head_dim], kj, NT_DIMS,
                                preferred_element_type=F32)
            if j == n_pages:
                row = lax.broadcasted_iota(jnp.int32, s.shape, 0)
                col = lax.broadcasted_iota(jnp.int32, s.shape, 1)
                s = jnp.where(col <= row, s, -jnp.inf)
            s_ref[hc * t:(hc + 1) * t, j * page:(j + 1) * page] = s

    s = s_ref[...]
    p = jnp.exp(s - jnp.max(s, axis=-1, keepdims=True))
    a = p / jnp.sum(p, axis=-1, keepdims=True)
    lam = _diff_lambda(lq1_ref, lk1_ref, lq2_ref, lk2_ref, lam_init)
    for h in range(heads):
        a_h = (a[2 * h * t:(2 * h + 1) * t] - lam * a[(2 * h + 1) * t:(2 * h + 2) * t]).astype(BF16)
        acc = jnp.zeros((t, hw), F32)
        for j in range(n_pages + 1):
            if j < n_pages:
                vj = jnp.concatenate([vc[j][pl.ds(half * heads + h, page, stride=n_hc), :] for half in range(2)],
                                     axis=1).astype(BF16)
            else:
                vj = vnp_ref[:, h * hw:(h + 1) * hw].astype(BF16)
            acc = acc + jnp.dot(a_h[:, j * page:(j + 1) * page], vj, preferred_element_type=F32)
        o_ref[:, h * hw:(h + 1) * hw] = _rms(acc, gsub_ref[...]) * (1.0 - lam_init)


def _sample_attention(page_table, lams, g_sub, q2d, k_rows, v2d, cache_k_rows, cache_v_rows, *, layer, n_phys,
                      page, dec_batch, t, heads, head_dim, lam_init):
    n_pages = page_table.shape[1]
    hw = 2 * head_dim
    n_hc = 2 * heads
    width = q2d.shape[1]
    vec = pl.BlockSpec((1, head_dim), lambda b, pt: (0, 0))
    row_blk = pl.BlockSpec((t, width), lambda b, pt: (b, 0))

    def page_spec(j):
        return pl.BlockSpec((page * n_hc, head_dim), lambda b, pt: (layer * n_phys + pt[b * n_pages + j], 0))

    pages = [page_spec(j) for j in range(n_pages)]
    grid_spec = pltpu.PrefetchScalarGridSpec(
        num_scalar_prefetch=1,
        grid=(dec_batch,),
        in_specs=[vec, vec, vec, vec, pl.BlockSpec((1, hw), lambda b, pt: (0, 0)), row_blk,
                  pl.BlockSpec((t * n_hc, head_dim), lambda b, pt: (b, 0)), row_blk]
        + pages + pages,
        out_specs=row_blk,
        scratch_shapes=[
            pltpu.VMEM((n_hc * t, (n_pages + 1) * page), F32),
            pltpu.VMEM((page * n_hc, head_dim), F32), pltpu.VMEM((page, width), F32),
        ],
    )
    return pl.pallas_call(
        functools.partial(_sample_attn_kernel, n_pages=n_pages, page=page, heads=heads, head_dim=head_dim,
                          lam_init=lam_init),
        grid_spec=grid_spec,
        out_shape=jax.ShapeDtypeStruct(q2d.shape, F32),
        compiler_params=_params(1),
        name="sample_attention",
    )(page_table.reshape(-1), *lams, g_sub, q2d, k_rows, v2d, *([cache_k_rows] * n_pages),
      *([cache_v_rows] * n_pages))


def _mix_out_kernel(x_ref, att_ref, u_ref, vg_ref, wm_ref, bs_ref, wo_ref, gffn_ref, wr_ref, br_ref, *rest,
                    blk, chunk, aliased):
    x1_ref, hn_ref, lg_ref, cm_ref = rest[3:] if aliased else rest
    tm = x_ref.shape[0]
    groups = wm_ref.shape[0]
    da_width = att_ref.shape[1]
    row = lax.broadcasted_iota(jnp.int32, (chunk, chunk), 0)
    col = lax.broadcasted_iota(jnp.int32, (chunk, chunk), 1)
    keep = (row // blk == col // blk) & (col <= row)
    for g in range(groups):
        gs = slice(g * chunk, (g + 1) * chunk)
        wm = jnp.where(keep, wm_ref[g], 0.0).astype(BF16)
        for c in range(tm // chunk):
            rs = slice(c * chunk, (c + 1) * chunk)
            s = jnp.dot(wm, vg_ref[rs, gs].astype(BF16), preferred_element_type=F32) + bs_ref[g]
            cm_ref[rs, gs] = (u_ref[rs, gs].astype(F32) * s).astype(BF16)
    y = jnp.dot(att_ref[...].astype(BF16), wo_ref[0:da_width, :], preferred_element_type=F32)
    y = y + jnp.dot(cm_ref[...], wo_ref[da_width:, :], preferred_element_type=F32)
    x1 = x_ref[...] + y
    x1_ref[...] = x1
    h = _rms(x1, gffn_ref[...])
    hn_ref[...] = h
    h_hi = h.astype(BF16)
    h_lo = (h - h_hi.astype(F32)).astype(BF16)
    lg = jnp.dot(h_hi, wr_ref[0], preferred_element_type=F32)
    lg = lg + jnp.dot(h_lo, wr_ref[0], preferred_element_type=F32)
    lg = lg + jnp.dot(h_hi, wr_ref[1], preferred_element_type=F32)
    lg_ref[...] = lg + br_ref[...]


def _mix_out(x2d, att, u, vg, wm, bs, wo_bf, g_ffn, wr, br, prev, *, tm, row_offset, n_total, blk, chunk, name):
    m, d = x2d.shape
    width = att.shape[1]
    groups = wm.shape[0]
    off = row_offset // tm
    row_blk = lambda w: pl.BlockSpec((tm, w), lambda i: (i, 0))
    fixed2 = lambda a: pl.BlockSpec(a.shape, lambda i: (0, 0))
    fixed3 = lambda a: pl.BlockSpec(a.shape, lambda i: (0, 0, 0))
    in_specs = [row_blk(d), row_blk(width), row_blk(width), row_blk(width), fixed3(wm), fixed3(bs),
                fixed2(wo_bf), fixed2(g_ffn), fixed3(wr), fixed2(br)]
    args = [x2d, att, u, vg, wm, bs, wo_bf, g_ffn, wr, br]
    aliases = {}
    if prev is not None:
        in_specs += [pl.BlockSpec(memory_space=pl.ANY)] * 3
        aliases = {len(args) + n: n for n in range(3)}
        args += list(prev)
    return pl.pallas_call(
        functools.partial(_mix_out_kernel, blk=blk, chunk=chunk, aliased=prev is not None),
        grid=(m // tm,),
        in_specs=in_specs,
        out_specs=[
            pl.BlockSpec((tm, d), lambda i: (i + off, 0)),
            pl.BlockSpec((tm, d), lambda i: (i + off, 0)),
            pl.BlockSpec((tm, LANES), lambda i: (i + off, 0)),
        ],
        out_shape=[
            jax.ShapeDtypeStruct((n_total, d), F32),
            jax.ShapeDtypeStruct((n_total, d), F32),
            jax.ShapeDtypeStruct((n_total, LANES), F32),
        ],
        scratch_shapes=[pltpu.VMEM((tm, groups * chunk), BF16)],
        input_output_aliases=aliases,
        compiler_params=_params(1),
        name=name,
    )(*args)


def _route_kernel(lg_ref, ri_ref, rw_ref, cnt_ref, carry_ref, *, n_groups, epg, exp_row0):
    tm = lg_ref.shape[0]
    n_exp = n_groups * epg

    @pl.when(pl.program_id(0) == 0)
    def _():
        carry_ref[...] = jnp.zeros(carry_ref.shape, F32)

    lt = lg_ref[...].T
    gl = lt[0:n_groups]
    gmax = jnp.max(gl, axis=0, keepdims=True)
    sub_g = lax.broadcasted_iota(jnp.int32, gl.shape, 0)
    gidx = jnp.min(jnp.where(gl == gmax, sub_g, n_groups), axis=0, keepdims=True)
    gw = 1.0 / jnp.sum(jnp.exp(gl - gmax), axis=0, keepdims=True)
    esel = jnp.zeros((epg, tm), F32)
    for gi in range(n_groups):
        esel = jnp.where(gidx == gi, lt[exp_row0 + gi * epg:exp_row0 + (gi + 1) * epg], esel)
    sub_e = lax.broadcasted_iota(jnp.int32, esel.shape, 0)
    v1 = jnp.max(esel, axis=0, keepdims=True)
    i1 = jnp.min(jnp.where(esel == v1, sub_e, epg), axis=0, keepdims=True)
    rest = jnp.where(sub_e == i1, -jnp.inf, esel)
    v2 = jnp.max(rest, axis=0, keepdims=True)
    i2 = jnp.min(jnp.where(rest == v2, sub_e, epg), axis=0, keepdims=True)
    tt = jnp.exp(v2 - v1)
    w1 = gw / (1.0 + tt)
    w2 = gw * tt / (1.0 + tt)
    e1 = gidx * epg + i1
    e2 = gidx * epg + i2

    sub_x = lax.broadcasted_iota(jnp.int32, (n_exp, tm), 0)
    oh1 = sub_x == e1
    oh2 = sub_x == e2
    oh = jnp.where(oh1, 1.0, 0.0) + jnp.where(oh2, 1.0, 0.0)
    earlier = lax.broadcasted_iota(jnp.int32, (tm, tm), 0) < lax.broadcasted_iota(jnp.int32, (tm, tm), 1)
    before = jnp.dot(oh.astype(BF16), jnp.where(earlier, 1.0, 0.0).astype(BF16),
                     preferred_element_type=F32) + carry_ref[...]
    r1 = jnp.sum(jnp.where(oh1, before, 0.0), axis=0, keepdims=True)
    r2 = jnp.sum(jnp.where(oh2, before, 0.0), axis=0, keepdims=True)
    carry_ref[...] = carry_ref[...] + jnp.sum(oh, axis=1, keepdims=True)

    zi = jnp.zeros((4, tm), jnp.int32)
    ri_ref[...] = jnp.concatenate([e1, e2, r1.astype(jnp.int32), r2.astype(jnp.int32), zi], axis=0)
    rw_ref[...] = jnp.concatenate([w1, w2, jnp.zeros((6, tm), F32)], axis=0)
    cnt_ref[...] = jnp.broadcast_to(carry_ref[...], cnt_ref.shape)


def _route(logits, *, tm, n_groups, epg, exp_row0):
    n = logits.shape[0]
    n_exp = n_groups * epg
    return pl.pallas_call(
        functools.partial(_route_kernel, n_groups=n_groups, epg=epg, exp_row0=exp_row0),
        grid=(n // tm,),
        in_specs=[pl.BlockSpec((tm, LANES), lambda i: (i, 0))],
        out_specs=[
            pl.BlockSpec((8, tm), lambda i: (0, i)),
            pl.BlockSpec((8, tm), lambda i: (0, i)),
            pl.BlockSpec((n_exp, LANES), lambda i: (0, 0)),
        ],
        out_shape=[
            jax.ShapeDtypeStruct((8, n), jnp.int32),
            jax.ShapeDtypeStruct((8, n), F32),
            jax.ShapeDtypeStruct((n_exp, LANES), F32),
        ],
        scratch_shapes=[pltpu.VMEM((n_exp, 1), F32)],
        compiler_params=_params(1),
        name="route",
    )(logits)


ISSUE_UNROLL = 8


def _dispatch_kernel(pos_ref, h_ref, xs_ref, sem, *, tm):
    def start(t, carry):
        for k in range(TOP_K):
            pos = pos_ref[k * tm + t]
            pltpu.make_async_copy(h_ref.at[pl.ds(t, 1), :], xs_ref.at[pl.ds(pos, 1), :], sem).start()
        return carry

    lax.fori_loop(0, tm, start, 0, unroll=ISSUE_UNROLL)
    for k in range(TOP_K):
        pltpu.make_async_copy(h_ref, xs_ref.at[pl.ds(0, tm), :], sem).wait()


def _dispatch(pos_flat, h, *, tm, n_rows):
    n, d = h.shape
    return pl.pallas_call(
        functools.partial(_dispatch_kernel, tm=tm),
        grid=(n // tm,),
        in_specs=[
            pl.BlockSpec((TOP_K * tm,), lambda i: (i,), memory_space=pltpu.SMEM),
            pl.BlockSpec((tm, d), lambda i: (i, 0)),
        ],
        out_specs=pl.BlockSpec(memory_space=pl.ANY),
        out_shape=jax.ShapeDtypeStruct((n_rows, d), F32),
        scratch_shapes=[pltpu.SemaphoreType.DMA(())],
        compiler_params=_params(1),
        name="dispatch",
    )(pos_flat, h)


def _expert_kernel(te_ref, nv_ref, nu_ref, xs_ref, w1_ref, w3_ref, w2_ref, ys_ref, w1b_ref, w3b_ref, w2b_ref):
    i = pl.program_id(0)
    tm = xs_ref.shape[0]
    changed = (i == 0) | (te_ref[i] != te_ref[jnp.maximum(i - 1, 0)])

    @pl.when(changed)
    def _():
        w1b_ref[...] = w1_ref[...].astype(BF16)
        w3b_ref[...] = w3_ref[...].astype(BF16)
        w2b_ref[...] = w2_ref[...].astype(BF16)

    nv = nv_ref[i]
    half = tm // 2

    def swiglu_rows(r0, rows):
        x = xs_ref[r0:r0 + rows, :]
        row = r0 + lax.broadcasted_iota(jnp.int32, x.shape, 0)
        x = jnp.where(row < nv, x, 0.0).astype(BF16)
        a1 = jnp.dot(x, w1b_ref[...], preferred_element_type=F32)
        a3 = jnp.dot(x, w3b_ref[...], preferred_element_type=F32)
        a = (a1 * (1.0 / (1.0 + jnp.exp(-a1))) * a3).astype(BF16)
        ys_ref[r0:r0 + rows, :] = jnp.dot(a, w2b_ref[...], preferred_element_type=F32)

    def zero_rows(r0, rows):
        ys_ref[r0:r0 + rows, :] = jnp.zeros((rows, ys_ref.shape[1]), F32)

    @pl.when(nv > half)
    def _():
        swiglu_rows(0, tm)

    @pl.when((nv > 0) & (nv <= half))
    def _():
        swiglu_rows(0, half)
        zero_rows(half, half)

    @pl.when(nv == 0)
    def _():
        zero_rows(0, tm)


def _experts(tile_expert, tile_valid, n_used, xs3, w1, w3, w2, *, layer, tm, n_tiles):
    d, de = w1.shape[2], w1.shape[3]
    w_in_spec = pl.BlockSpec((None, None, d, de), lambda i, te, nv, nu: (layer, te[i], 0, 0))
    grid_spec = pltpu.PrefetchScalarGridSpec(
        num_scalar_prefetch=3,
        grid=(n_tiles,),
        in_specs=[
            pl.BlockSpec((tm, d), lambda i, te, nv, nu: (jnp.minimum(i, nu[0] - 1), 0)),
            w_in_spec, w_in_spec,
            pl.BlockSpec((None, None, de, d), lambda i, te, nv, nu: (layer, te[i], 0, 0)),
        ],
        out_specs=pl.BlockSpec((tm, d), lambda i, te, nv, nu: (jnp.minimum(i, nu[0]), 0)),
        scratch_shapes=[pltpu.VMEM((d, de), BF16), pltpu.VMEM((d, de), BF16), pltpu.VMEM((de, d), BF16)],
    )
    return pl.pallas_call(
        _expert_kernel,
        grid_spec=grid_spec,
        out_shape=jax.ShapeDtypeStruct(xs3.shape, F32),
        compiler_params=_params(1, 56 * 1024 * 1024),
        name="experts",
    )(tile_expert, tile_valid, n_used, xs3, w1, w3, w2)


def _combine_kernel(pos_ref, pos_next_ref, x1_ref, rw_ref, gfin_ref, ys_ref, o_ref, buf_ref, sem, *, tm):
    i = pl.program_id(0)
    slot = i % 2

    def gather(p_ref, s):
        def start(t, carry):
            for k in range(TOP_K):
                pos = p_ref[k * tm + t]
                pltpu.make_async_copy(ys_ref.at[pl.ds(pos, 1), :],
                                      buf_ref.at[s, k, pl.ds(t, 1), :], sem.at[s]).start()
            return carry
        lax.fori_loop(0, tm, start, 0, unroll=ISSUE_UNROLL)

    @pl.when(i == 0)
    def _():
        gather(pos_ref, 0)

    @pl.when(i + 1 < pl.num_programs(0))
    def _():
        gather(pos_next_ref, 1 - slot)

    for k in range(TOP_K):
        pltpu.make_async_copy(ys_ref.at[pl.ds(0, tm), :], buf_ref.at[slot, k], sem.at[slot]).wait()

    x = x1_ref[...]
    for k in range(TOP_K):
        x = x + rw_ref[:, k:k + 1] * buf_ref[slot, k]
    o_ref[...] = _rms(x, gfin_ref[...])


def _combine(pos_flat, x1, rw_cols, g_final, ys3, *, tm, row_offset, rows):
    d = x1.shape[1]
    off = row_offset // tm
    last = off + rows // tm - 1
    return pl.pallas_call(
        functools.partial(_combine_kernel, tm=tm),
        grid=(rows // tm,),
        in_specs=[
            pl.BlockSpec((TOP_K * tm,), lambda i: (i + off,), memory_space=pltpu.SMEM),
            pl.BlockSpec((TOP_K * tm,), lambda i: (jnp.minimum(i + off + 1, last),), memory_space=pltpu.SMEM),
            pl.BlockSpec((tm, d), lambda i: (i + off, 0)),
            pl.BlockSpec((tm, TOP_K), lambda i: (i + off, 0)),
            pl.BlockSpec((1, d), lambda i: (0, 0)),
            pl.BlockSpec(memory_space=pl.ANY),
        ],
        out_specs=pl.BlockSpec((tm, d), lambda i: (i, 0)),
        out_shape=jax.ShapeDtypeStruct((rows, d), F32),
        scratch_shapes=[pltpu.VMEM((2, TOP_K, tm, d), F32), pltpu.SemaphoreType.DMA((2,))],
        compiler_params=_params(1),
        name="combine",
    )(pos_flat, pos_flat, x1, rw_cols, g_final, ys3)


def _moe(x1, h_norm, logits, w1, w3, w2, g_final, *, layer, n_groups, epg, exp_row0, tm_route, tm_row, tm_exp):
    n_tok = x1.shape[0]
    n_exp = n_groups * epg
    ri, rw, cnt = _route(logits, tm=tm_route, n_groups=n_groups, epg=epg, exp_row0=exp_row0)
    counts = cnt[:, 0].astype(jnp.int32)
    n_tiles = (n_tok * TOP_K) // tm_exp + n_exp
    tiles_e = (counts + tm_exp - 1) // tm_exp
    tile_end = jnp.cumsum(tiles_e)
    tile_start = tile_end - tiles_e
    n_used = tile_end[-1]
    tile_id = jnp.arange(n_tiles, dtype=jnp.int32)
    tile_e = jnp.sum((tile_id[:, None] >= tile_end[None, :]).astype(jnp.int32), axis=1)
    tile_e = jnp.minimum(tile_e, n_exp - 1)
    tile_e = jnp.where(tile_id < n_used, tile_e, tile_e[jnp.maximum(n_used - 1, 0)])
    tile_valid = jnp.clip(counts[tile_e] - (tile_id - tile_start[tile_e]) * tm_exp, 0, tm_exp)
    tile_valid = jnp.where(tile_id < n_used, tile_valid, 0).astype(jnp.int32)
    offsets = (tile_start * tm_exp).astype(jnp.int32)
    same = ri[:TOP_K, :, None] == jnp.arange(n_exp, dtype=jnp.int32)
    pos = jnp.sum(jnp.where(same, offsets, 0), axis=-1) + ri[TOP_K:2 * TOP_K]
    pos_flat = pos.reshape(TOP_K, n_tok // tm_row, tm_row).transpose(1, 0, 2).reshape(-1)
    rw_cols = rw[:TOP_K].T

    xs3 = _dispatch(pos_flat, h_norm, tm=tm_row, n_rows=n_tiles * tm_exp)
    ys3 = _experts(tile_e, tile_valid, n_used.reshape(1).astype(jnp.int32), xs3, w1, w3, w2, layer=layer,
                   tm=tm_exp, n_tiles=n_tiles)
    return functools.partial(_combine, pos_flat, x1, rw_cols, g_final.reshape(1, -1), ys3, tm=tm_row)


def kernel(x_prompt, x_sample, cache_k, cache_v, page_table, g_attn, w_in, lambda_q1, lambda_k1, lambda_q2,
           lambda_k2, g_subln, ln_v_g, ln_v_b, w_spatial, b_spatial, w_out, g_ffn, w_router_grp, b_router_grp,
           w_router_exp, b_router_exp, w1, w3, w2, g_final):
    batch, seq, d = x_prompt.shape
    dec_batch, t, _ = x_sample.shape
    depth, n_phys, page, heads, _, head_dim = cache_k.shape
    groups, chunk = w_spatial.shape[1], w_spatial.shape[2]
    n_groups, n_exp = w_router_grp.shape[2], w_router_exp.shape[2]
    epg = n_exp // n_groups
    n_pages = page_table.shape[1]
    past = n_pages * page
    sec = w_in.shape[2] // 5
    n_p, n_s = batch * seq, dec_batch * t
    n_tok = n_p + n_s
    assert depth == 1 and head_dim == LANES
    assert sec == heads * 2 * head_dim == groups * chunk and chunk % t == 0

    l = 0
    lam_init = 0.8 - 0.6 * math.exp(-0.3 * l)
    tm_in, tq, tm_mix, tm_route, tm_row, tm_exp = 512, 256, 256, 512, 512, 256

    xp = x_prompt.reshape(n_p, d)
    xs = x_sample.reshape(n_s, d)
    row = lambda a: a[l].reshape(1, -1)
    lams = [row(a) for a in (lambda_q1, lambda_k1, lambda_q2, lambda_k2)]
    w_in_bf = w_in[l].astype(BF16)
    w_out_bf = w_out[l].astype(BF16)

    cos_p, sin_p = _rope_tables(seq, seq, 0, head_dim)
    cos_s, sin_s = _rope_tables(tm_in, t, past, head_dim)
    proj = functools.partial(_in_projection, g=row(g_attn), w_bf=w_in_bf, ln_g=row(ln_v_g), ln_b=row(ln_v_b),
                             tm=tm_in, head_dim=head_dim)
    q_p, k8_p, kb_p, v_p, vb_p, u_p, vg_p = proj(xp, cos_t=cos_p, sin_t=sin_p, table_blocks=seq // tm_in,
                                                 name="in_proj_prompt")
    q_s, k8_s, _, v_s, _, u_s, vg_s = proj(xs, cos_t=cos_s, sin_t=sin_s, table_blocks=1, name="in_proj_sample")

    att_p = _prompt_attention(lams, g_subln[l].reshape(-1, 1), q_p, kb_p, vb_p, batch=batch, seq=seq, heads=heads,
                              head_dim=head_dim, tq=tq, lam_init=lam_init)
    cache_k_rows = cache_k.reshape(-1, head_dim)
    cache_v_rows = cache_v.reshape(depth, n_phys, page, heads, 2, head_dim).transpose(0, 1, 2, 4, 3, 5)
    att_s = _sample_attention(page_table, lams, row(g_subln), q_s, k8_s, v_s, cache_k_rows,
                              cache_v_rows.reshape(-1, head_dim), layer=l, n_phys=n_phys, page=page,
                              dec_batch=dec_batch, t=t, heads=heads, head_dim=head_dim, lam_init=lam_init)

    exp_row0 = 8
    wr = jnp.zeros((d, LANES), F32).at[:, :n_groups].set(w_router_grp[l])
    wr = wr.at[:, exp_row0:exp_row0 + n_exp].set(w_router_exp[l])
    wr_hi = wr.astype(BF16)
    wr = jnp.stack([wr_hi, (wr - wr_hi.astype(F32)).astype(BF16)])
    br = jnp.zeros((1, LANES), F32).at[0, :n_groups].set(b_router_grp[l])
    br = br.at[0, exp_row0:exp_row0 + n_exp].set(b_router_exp[l])
    reps = chunk // t
    wm_s = jnp.tile(w_spatial[l][:, :t, :t], (1, reps, reps))
    bs_s = jnp.tile(b_spatial[l][:, :t], (1, reps))[..., None]
    mix = functools.partial(_mix_out, wo_bf=w_out_bf, g_ffn=row(g_ffn), wr=wr, br=br, n_total=n_tok, chunk=chunk)
    bufs = mix(xp, att_p, u_p, vg_p, w_spatial[l], b_spatial[l][..., None], prev=None, tm=tm_mix, row_offset=0,
               blk=chunk, name="mix_out_prompt")
    x1, h_norm, logits = mix(xs, att_s, u_s, vg_s, wm_s, bs_s, prev=bufs, tm=tm_mix, row_offset=n_p, blk=t,
                         name="mix_out_sample")

    comb = _moe(x1, h_norm, logits, w1, w3, w2, g_final, layer=l, n_groups=n_groups, epg=epg, exp_row0=exp_row0,
                tm_route=tm_route, tm_row=tm_row, tm_exp=tm_exp)
    y_p = comb(row_offset=0, rows=n_p)
    y_s = comb(row_offset=n_p, rows=n_s)

    kv6 = lambda a, b_: a.reshape(depth, b_, -1, heads, 2, head_dim)
    v5 = lambda a, b_: a.reshape(depth, b_, -1, heads, 2 * head_dim)
    return (y_p.reshape(batch, seq, d), y_s.reshape(dec_batch, t, d),
            kv6(k8_p, batch), v5(v_p, batch), kv6(k8_s, dec_batch), v5(v_s, dec_batch),
            vg_p.reshape(batch, seq, -1)[None, :, seq - chunk:], vg_s.reshape(depth, dec_batch, t, -1))
```

```python
import functools
import math

import jax
import jax.numpy as jnp
from jax import lax
from jax.experimental import pallas as pl
from jax.experimental.pallas import tpu as pltpu

F32 = jnp.float32
BF16 = jnp.bfloat16

ROPE_THETA = 10000.0
RMS_EPS = 1e-6
LN_EPS = 1e-5
TOP_K = 2

LANES = 128
VMEM_LIMIT = 52 * 1024 * 1024

NT_DIMS = (((1,), (1,)), ((), ()))


def _params(n_axes, vmem=VMEM_LIMIT):
    return pltpu.CompilerParams(dimension_semantics=("arbitrary",) * n_axes, vmem_limit_bytes=vmem)


def _rms(x, g):
    return x * lax.rsqrt(jnp.mean(x * x, axis=-1, keepdims=True) + RMS_EPS) * g


def _rope_table_kernel(cos_ref, sin_ref, *, period, offset, head_dim):
    rows, lanes = cos_ref.shape
    half = head_dim // 2
    row = lax.broadcasted_iota(jnp.int32, (rows, lanes), 0)
    lane = lax.broadcasted_iota(jnp.int32, (rows, lanes), 1)
    pos = (offset + row % period).astype(F32)
    j = (lane % half).astype(F32)
    inv_freq = jnp.exp(-math.log(ROPE_THETA) * j * (2.0 / head_dim))
    ang = pos * inv_freq
    cos_ref[...] = jnp.cos(ang)
    s = jnp.sin(ang)
    sin_ref[...] = jnp.where(lane < half, -s, s)


def _rope_tables(rows, period, offset, head_dim):
    return pl.pallas_call(
        functools.partial(_rope_table_kernel, period=period, offset=offset, head_dim=head_dim),
        out_shape=(jax.ShapeDtypeStruct((rows, head_dim), F32),) * 2,
        name="rope_tables",
    )()


def _in_proj_kernel(x_ref, g_ref, w_hbm, cos_ref, sin_ref, lng_ref, lnb_ref,
                    q_ref, k8_ref, kb_ref, v_ref, vb_ref, u_ref, vg_ref, w_ref, xn_ref, sem, *, q_scale, head_dim):
    tm = x_ref.shape[0]
    heads = v_ref.shape[1]
    hw = v_ref.shape[2]
    sec = q_ref.shape[1]
    n_head_cols = sec // head_dim

    @pl.when(pl.program_id(0) == 0)
    def _():
        load = pltpu.make_async_copy(w_hbm, w_ref, sem)
        load.start()
        load.wait()

    xn_ref[...] = _rms(x_ref[...], g_ref[...]).astype(BF16)

    def project(section):
        return jnp.dot(xn_ref[...], w_ref[:, section * sec:(section + 1) * sec], preferred_element_type=F32)

    def rope(z, c):
        zc = z[:, c * head_dim:(c + 1) * head_dim]
        return zc * cos_ref[...] + pltpu.roll(zc, head_dim // 2, axis=1) * sin_ref[...]

    def gelu(t):
        return 0.5 * t * (1.0 + lax.erf(t * math.sqrt(0.5)))

    z = project(0)
    for c in range(n_head_cols):
        q_ref[:, c * head_dim:(c + 1) * head_dim] = rope(z, c) * q_scale

    z = project(1)
    for c in range(n_head_cols):
        r = rope(z, c)
        k8_ref[pl.ds(c, tm, stride=n_head_cols), :] = r
        kb_ref[:, c * head_dim:(c + 1) * head_dim] = r.astype(BF16)

    z = project(2)
    for h in range(heads):
        v_ref[:, h, :] = z[:, h * hw:(h + 1) * hw]
    vb_ref[...] = z.astype(BF16)

    u_ref[...] = gelu(project(3)).astype(u_ref.dtype)

    c = gelu(project(4))
    cc = c - jnp.mean(c, axis=-1, keepdims=True)
    var = jnp.mean(cc * cc, axis=-1, keepdims=True)
    vg_ref[...] = cc * lax.rsqrt(var + LN_EPS) * lng_ref[...] + lnb_ref[...]


def _in_projection(x2d, g, w_bf, cos_t, sin_t, ln_g, ln_b, *, tm, table_blocks, heads, head_dim, name):
    m, d = x2d.shape
    sec = w_bf.shape[1] // 5
    hc = sec // head_dim
    hw = sec // heads
    row_blk = lambda i: (i, 0)
    fixed = lambda i: (0, 0)
    table = lambda i: (i % table_blocks, 0)
    wide = lambda dt: (pl.BlockSpec((tm, sec), row_blk), jax.ShapeDtypeStruct((m, sec), dt))
    outs = [wide(F32),
            (pl.BlockSpec((tm * hc, head_dim), row_blk), jax.ShapeDtypeStruct((m * hc, head_dim), F32)),
            wide(BF16),
            (pl.BlockSpec((tm, heads, hw), lambda i: (i, 0, 0)), jax.ShapeDtypeStruct((m, heads, hw), F32)),
            wide(BF16), wide(BF16), wide(F32)]
    return pl.pallas_call(
        functools.partial(_in_proj_kernel, q_scale=head_dim ** -0.5, head_dim=head_dim),
        grid=(m // tm,),
        in_specs=[
            pl.BlockSpec((tm, d), row_blk),
            pl.BlockSpec((1, d), fixed),
            pl.BlockSpec(memory_space=pl.ANY),
            pl.BlockSpec((tm, head_dim), table),
            pl.BlockSpec((tm, head_dim), table),
            pl.BlockSpec((1, sec), fixed),
            pl.BlockSpec((1, sec), fixed),
        ],
        out_specs=[o[0] for o in outs],
        out_shape=[o[1] for o in outs],
        scratch_shapes=[pltpu.VMEM(w_bf.shape, BF16), pltpu.VMEM((tm, d), BF16), pltpu.SemaphoreType.DMA(())],
        compiler_params=_params(1),
        name=name,
    )(x2d, g, w_bf, cos_t, sin_t, ln_g, ln_b)


def _diff_lambda(lq1_ref, lk1_ref, lq2_ref, lk2_ref, lam_init):
    a = jnp.sum(lq1_ref[...] * lk1_ref[...], axis=-1, keepdims=True)
    b = jnp.sum(lq2_ref[...] * lk2_ref[...], axis=-1, keepdims=True)
    return jnp.exp(a) - jnp.exp(b) + lam_init


def _prompt_attn_kernel(lq1_ref, lk1_ref, lq2_ref, lk2_ref, gsub_ref, q_ref, k_ref, v_ref, o_ref,
                        vt_ref, *, tq, head_dim, lam_init):
    seq = q_ref.shape[0]
    for kb in range(seq // tq):
        vt_ref[:, kb * tq:(kb + 1) * tq] = v_ref[kb * tq:(kb + 1) * tq, :].astype(F32).T.astype(BF16)
    lam =_diff_lambda(lq1_ref, lk1_ref, lq2_ref, lk2_ref, lam_init)
    key = lax.broadcasted_iota(jnp.int32, (tq, tq), 0)
    qry = lax.broadcasted_iota(jnp.int32, (tq, tq), 1)
    causal = key <= qry

    for qi in range(seq // tq):
        n_keys = (qi + 1) * tq
        q_t = q_ref[qi * tq:(qi + 1) * tq, :].T.astype(BF16)
        outs = []
        for c in range(2):
            sl = slice(c * head_dim, (c + 1) * head_dim)
            s = jnp.dot(k_ref[0:n_keys, sl], q_t[sl, :], preferred_element_type=F32)
            diag = jnp.where(causal, s[qi * tq:, :], -jnp.inf)
            s = diag if qi == 0 else jnp.concatenate([s[:qi * tq, :], diag], axis=0)
            p = jnp.exp(s - jnp.max(s, axis=0, keepdims=True))
            denom = jnp.sum(p, axis=0, keepdims=True)
            acc = jnp.dot(vt_ref[:, 0:n_keys], p.astype(BF16), preferred_element_type=F32)
            outs.append(acc / denom)
        o_t = outs[0] - lam * outs[1]
        ms = jnp.mean(o_t * o_t, axis=0, keepdims=True)
        o_t = o_t * lax.rsqrt(ms + RMS_EPS) * gsub_ref[...] * (1.0 - lam_init)
        o_ref[qi * tq:(qi + 1) * tq, :] = o_t.T.astype(o_ref.dtype)


def _prompt_attention(lams, g_sub_col, q2d, k2d, v2d, *, batch, seq, heads, head_dim, tq, lam_init):
    hw = 2 * head_dim
    vec = pl.BlockSpec((1, head_dim), lambda b, h: (0, 0))
    blk = pl.BlockSpec((seq, hw), lambda b, h: (b, h))
    return pl.pallas_call(
        functools.partial(_prompt_attn_kernel, tq=tq, head_dim=head_dim, lam_init=lam_init),
        grid=(batch, heads),
        in_specs=[vec, vec, vec, vec, pl.BlockSpec((hw, 1), lambda b, h: (0, 0)), blk, blk, blk],
        out_specs=blk,
        out_shape=jax.ShapeDtypeStruct(q2d.shape, BF16),
        scratch_shapes=[pltpu.VMEM((hw, seq), BF16)],
        compiler_params=_params(2),
        name="prompt_attention",
    )(*lams, g_sub_col, q2d, k2d, v2d)


def _sample_attn_kernel(pt_ref, lq1_ref, lk1_ref, lq2_ref, lk2_ref, gsub_ref, q_ref, kn_ref, vn_ref, *rest,
                        n_pages, page, heads, head_dim, lam_init):
    kc = rest[:n_pages]
    vc = rest[n_pages:2 * n_pages]
    o_ref, s_ref, knp_ref, vnp_ref = rest[2 * n_pages:]
    t = q_ref.shape[0]
    hw = 2 * head_dim
    n_hc = 2 * heads

    @pl.when(pl.program_id(0) == 0)
    def _():
        knp_ref[...] = jnp.zeros(knp_ref.shape, F32)
        vnp_ref[...] = jnp.zeros(vnp_ref.shape, F32)

    knp_ref[0:t * n_hc, :] = kn_ref[...]
    for h in range(heads):
        vnp_ref[0:t, h * hw:(h + 1) * hw] = vn_ref[:, h, :]

    q = q_ref[...].astype(BF16)
    for j in range(n_pages + 1):
        src = kc[j] if j < n_pages else knp_ref
        for hc in range(n_hc):
            kj = src[pl.ds(hc, page, stride=n_hc), :].astype(BF16)
            s = lax.dot_general(q[:, hc * head_dim:(hc + 1) * head_dim], kj, NT_DIMS,
                                preferred_element_type=F32)
            if j == n_pages:
                row = lax.broadcasted_iota(jnp.int32, s.shape, 0)
                col = lax.broadcasted_iota(jnp.int32, s.shape, 1)
                s = jnp.where(col <= row, s, -jnp.inf)
            s_ref[hc * t:(hc + 1) * t, j * page:(j + 1) * page] = s

    s = s_ref[...]
    p = jnp.exp(s - jnp.max(s, axis=-1, keepdims=True))
    a = p / jnp.sum(p, axis=-1, keepdims=True)
    lam = _diff_lambda(lq1_ref, lk1_ref, lq2_ref, lk2_ref, lam_init)
    for h in range(heads):
        a_h = (a[2 * h * t:(2 * h + 1) * t] - lam * a[(2 * h + 1) * t:(2 * h + 2) * t]).astype(BF16)
        acc = jnp.zeros((t, hw), F32)
        for j in range(n_pages + 1):
            if j < n_pages:
                vj = jnp.concatenate([vc[j][pl.ds(half * heads + h, page, stride=n_hc), :] for half in range(2)],
                                     axis=1).astype(BF16)
            else:
                vj = vnp_ref[:, h * hw:(h + 1) * hw].astype(BF16)
            acc = acc + jnp.dot(a_h[:, j * page:(j + 1) * page], vj, preferred_element_type=F32)
        o_ref[:, h * hw:(h + 1) * hw] = _rms(acc, gsub_ref[...]) * (1.0 - lam_init)


def _sample_attention(page_table, lams, g_sub, q2d, k_rows, v2d, cache_k_rows, cache_v_rows, *, layer, n_phys,
                      page, dec_batch, t, heads, head_dim, lam_init):
    n_pages = page_table.shape[1]
    hw = 2 * head_dim
    n_hc = 2 * heads
    width = q2d.shape[1]
    vec = pl.BlockSpec((1, head_dim), lambda b, pt: (0, 0))
    row_blk = pl.BlockSpec((t, width), lambda b, pt: (b, 0))

    def page_spec(j):
        return pl.BlockSpec((page * n_hc, head_dim), lambda b, pt: (layer * n_phys + pt[b * n_pages + j], 0))

    pages = [page_spec(j) for j in range(n_pages)]
    grid_spec = pltpu.PrefetchScalarGridSpec(
        num_scalar_prefetch=1,
        grid=(dec_batch,),
        in_specs=[vec, vec, vec, vec, pl.BlockSpec((1, hw), lambda b, pt: (0, 0)), row_blk,
                  pl.BlockSpec((t * n_hc, head_dim), lambda b, pt: (b, 0)),
                  pl.BlockSpec((t, heads, hw), lambda b, pt: (b, 0, 0))]
        + pages + pages,
        out_specs=row_blk,
        scratch_shapes=[
            pltpu.VMEM((n_hc * t, (n_pages + 1) * page), F32),
            pltpu.VMEM((page * n_hc, head_dim), F32), pltpu.VMEM((page, width), F32),
        ],
    )
    return pl.pallas_call(
        functools.partial(_sample_attn_kernel, n_pages=n_pages, page=page, heads=heads, head_dim=head_dim,
                          lam_init=lam_init),
        grid_spec=grid_spec,
        out_shape=jax.ShapeDtypeStruct(q2d.shape, F32),
        compiler_params=_params(1),
        name="sample_attention",
    )(page_table.reshape(-1), *lams, g_sub, q2d, k_rows, v2d, *([cache_k_rows] * n_pages),
      *([cache_v_rows] * n_pages))


def _mix_out_kernel(x_ref, att_ref, u_ref, vg_ref, wm_ref, bs_ref, wo_ref, gffn_ref, wr_ref, br_ref, *rest,
                    blk, chunk, aliased):
    x1_ref, hn_ref, lg_ref, cm_ref = rest[3:] if aliased else rest
    tm = x_ref.shape[0]
    groups = wm_ref.shape[0]
    da_width = att_ref.shape[1]
    row = lax.broadcasted_iota(jnp.int32, (chunk, chunk), 0)
    col = lax.broadcasted_iota(jnp.int32, (chunk, chunk), 1)
    keep = (row // blk == col // blk) & (col <= row)
    for g in range(groups):
        gs = slice(g * chunk, (g + 1) * chunk)
        wm = jnp.where(keep, wm_ref[g], 0.0).astype(BF16)
        for c in range(tm // chunk):
            rs = slice(c * chunk, (c + 1) * chunk)
            s = jnp.dot(wm, vg_ref[rs, gs].astype(BF16), preferred_element_type=F32) + bs_ref[g]
            cm_ref[rs, gs] = (u_ref[rs, gs].astype(F32) * s).astype(BF16)
    y = jnp.dot(att_ref[...].astype(BF16), wo_ref[0:da_width, :], preferred_element_type=F32)
    y = y + jnp.dot(cm_ref[...], wo_ref[da_width:, :], preferred_element_type=F32)
    x1 = x_ref[...] + y
    x1_ref[...] = x1
    h = _rms(x1, gffn_ref[...])
    hn_ref[...] = h
    h_hi = h.astype(BF16)
    h_lo = (h - h_hi.astype(F32)).astype(BF16)
    lg = jnp.dot(h_hi, wr_ref[0], preferred_element_type=F32)
    lg = lg + jnp.dot(h_lo, wr_ref[0], preferred_element_type=F32)
    lg = lg + jnp.dot(h_hi, wr_ref[1], preferred_element_type=F32)
    lg_ref[...] = lg + br_ref[...]


def _mix_out(x2d, att, u, vg, wm, bs, wo_bf, g_ffn, wr, br, prev, *, tm, row_offset, n_total, blk, chunk, name):
    m, d = x2d.shape
    width = att.shape[1]
    groups = wm.shape[0]
    off = row_offset // tm
    row_blk = lambda w: pl.BlockSpec((tm, w), lambda i: (i, 0))
    fixed2 = lambda a: pl.BlockSpec(a.shape, lambda i: (0, 0))
    fixed3 = lambda a: pl.BlockSpec(a.shape, lambda i: (0, 0, 0))
    in_specs = [row_blk(d), row_blk(width), row_blk(width), row_blk(width), fixed3(wm), fixed3(bs),
                fixed2(wo_bf), fixed2(g_ffn), fixed3(wr), fixed2(br)]
    args = [x2d, att, u, vg, wm, bs, wo_bf, g_ffn, wr, br]
    aliases = {}
    if prev is not None:
        in_specs += [pl.BlockSpec(memory_space=pl.ANY)] * 3
        aliases = {len(args) + n: n for n in range(3)}
        args += list(prev)
    return pl.pallas_call(
        functools.partial(_mix_out_kernel, blk=blk, chunk=chunk, aliased=prev is not None),
        grid=(m // tm,),
        in_specs=in_specs,
        out_specs=[
            pl.BlockSpec((tm, d), lambda i: (i + off, 0)),
            pl.BlockSpec((tm, d), lambda i: (i + off, 0)),
            pl.BlockSpec((tm, LANES), lambda i: (i + off, 0)),
        ],
        out_shape=[
            jax.ShapeDtypeStruct((n_total, d), F32),
            jax.ShapeDtypeStruct((n_total, d), F32),
            jax.ShapeDtypeStruct((n_total, LANES), F32),
        ],
        scratch_shapes=[pltpu.VMEM((tm, groups * chunk), BF16)],
        input_output_aliases=aliases,
        compiler_params=_params(1),
        name=name,
    )(*args)


def _route_kernel(lg_ref, ri_ref, rw_ref, cnt_ref, carry_ref, *, n_groups, epg, exp_row0):
    tm = lg_ref.shape[0]
    n_exp = n_groups * epg

    @pl.when(pl.program_id(0) == 0)
    def _():
        carry_ref[...] = jnp.zeros(carry_ref.shape, F32)

    lt = lg_ref[...].T
    gl = lt[0:n_groups]
    gmax = jnp.max(gl, axis=0, keepdims=True)
    sub_g = lax.broadcasted_iota(jnp.int32, gl.shape, 0)
    gidx = jnp.min(jnp.where(gl == gmax, sub_g, n_groups), axis=0, keepdims=True)
    gw = 1.0 / jnp.sum(jnp.exp(gl - gmax), axis=0, keepdims=True)
    esel = jnp.zeros((epg, tm), F32)
    for gi in range(n_groups):
        esel = jnp.where(gidx == gi, lt[exp_row0 + gi * epg:exp_row0 + (gi + 1) * epg], esel)
    sub_e = lax.broadcasted_iota(jnp.int32, esel.shape, 0)
    v1 = jnp.max(esel, axis=0, keepdims=True)
    i1 = jnp.min(jnp.where(esel == v1, sub_e, epg), axis=0, keepdims=True)
    rest = jnp.where(sub_e == i1, -jnp.inf, esel)
    v2 = jnp.max(rest, axis=0, keepdims=True)
    i2 = jnp.min(jnp.where(rest == v2, sub_e, epg), axis=0, keepdims=True)
    tt = jnp.exp(v2 - v1)
    w1 = gw / (1.0 + tt)
    w2 = gw * tt / (1.0 + tt)
    e1 = gidx * epg + i1
    e2 = gidx * epg + i2

    sub_x = lax.broadcasted_iota(jnp.int32, (n_exp, tm), 0)
    oh1 = sub_x == e1
    oh2 = sub_x == e2
    oh = jnp.where(oh1, 1.0, 0.0) + jnp.where(oh2, 1.0, 0.0)
    earlier = lax.broadcasted_iota(jnp.int32, (tm, tm), 0) < lax.broadcasted_iota(jnp.int32, (tm, tm), 1)
    before = jnp.dot(oh.astype(BF16), jnp.where(earlier, 1.0, 0.0).astype(BF16),
                     preferred_element_type=F32) + carry_ref[...]
    r1 = jnp.sum(jnp.where(oh1, before, 0.0), axis=0, keepdims=True)
    r2 = jnp.sum(jnp.where(oh2, before, 0.0), axis=0, keepdims=True)
    carry_ref[...] = carry_ref[...] + jnp.sum(oh, axis=1, keepdims=True)

    zi = jnp.zeros((4, tm), jnp.int32)
    ri_ref[...] = jnp.concatenate([e1, e2, r1.astype(jnp.int32), r2.astype(jnp.int32), zi], axis=0)
    rw_ref[...] = jnp.concatenate([w1, w2, jnp.zeros((6, tm), F32)], axis=0)
    cnt_ref[...] = jnp.broadcast_to(carry_ref[...], cnt_ref.shape)


def _route(logits, *, tm, n_groups, epg, exp_row0):
    n = logits.shape[0]
    n_exp = n_groups * epg
    return pl.pallas_call(
        functools.partial(_route_kernel, n_groups=n_groups, epg=epg, exp_row0=exp_row0),
        grid=(n // tm,),
        in_specs=[pl.BlockSpec((tm, LANES), lambda i: (i, 0))],
        out_specs=[
            pl.BlockSpec((8, tm), lambda i: (0, i)),
            pl.BlockSpec((8, tm), lambda i: (0, i)),
            pl.BlockSpec((n_exp, LANES), lambda i: (0, 0)),
        ],
        out_shape=[
            jax.ShapeDtypeStruct((8, n), jnp.int32),
            jax.ShapeDtypeStruct((8, n), F32),
            jax.ShapeDtypeStruct((n_exp, LANES), F32),
        ],
        scratch_shapes=[pltpu.VMEM((n_exp, 1), F32)],
        compiler_params=_params(1),
        name="route",
    )(logits)


ISSUE_UNROLL = 8


def _dispatch_kernel(pos_ref, h_ref, xs_ref, sem, *, tm):
    def start(t, carry):
        for k in range(TOP_K):
            pos = pos_ref[k * tm + t]
            pltpu.make_async_copy(h_ref.at[pl.ds(t, 1), :], xs_ref.at[pl.ds(pos, 1), :], sem).start()
        return carry

    lax.fori_loop(0, tm, start, 0, unroll=ISSUE_UNROLL)
    for k in range(TOP_K):
        pltpu.make_async_copy(h_ref, xs_ref.at[pl.ds(0, tm), :], sem).wait()


def _dispatch(pos_flat, h, *, tm, n_rows):
    n, d = h.shape
    return pl.pallas_call(
        functools.partial(_dispatch_kernel, tm=tm),
        grid=(n // tm,),
        in_specs=[
            pl.BlockSpec((TOP_K * tm,), lambda i: (i,), memory_space=pltpu.SMEM),
            pl.BlockSpec((tm, d), lambda i: (i, 0)),
        ],
        out_specs=pl.BlockSpec(memory_space=pl.ANY),
        out_shape=jax.ShapeDtypeStruct((n_rows, d), F32),
        scratch_shapes=[pltpu.SemaphoreType.DMA(())],
        compiler_params=_params(1),
        name="dispatch",
    )(pos_flat, h)


def _expert_kernel(te_ref, nv_ref, nu_ref, nxt_ref, xs_ref, w1_hbm, w3_hbm, w2_hbm, ys_ref,
                   st1_ref, st3_ref, st2_ref, w1b_ref, w3b_ref, w2b_ref, sem, *, layer):
    i = pl.program_id(0)
    tm = xs_ref.shape[0]
    weights = ((w1_hbm, st1_ref, w1b_ref), (w3_hbm, st3_ref, w3b_ref), (w2_hbm, st2_ref, w2b_ref))

    def load(expert, n):
        return pltpu.make_async_copy(weights[n][0].at[layer, expert], weights[n][1], sem.at[n])

    @pl.when(i == 0)
    def _():
        for n in range(3):
            load(te_ref[0], n).start()

    first_of_group = (i == 0) | (te_ref[i] != te_ref[jnp.maximum(i - 1, 0)])

    @pl.when(first_of_group)
    def _():
        for n in range(3):
            load(te_ref[i], n).wait()
            weights[n][2][...] = weights[n][1][...].astype(BF16)

        @pl.when(nxt_ref[i] >= 0)
        def _():
            for n in range(3):
                load(nxt_ref[i], n).start()

    nv = nv_ref[i]
    half = tm // 2

    def swiglu_rows(r0, rows):
        x = xs_ref[r0:r0 + rows, :]
        row = r0 + lax.broadcasted_iota(jnp.int32, x.shape, 0)
        x = jnp.where(row < nv, x, 0.0).astype(BF16)
        a1 = jnp.dot(x, w1b_ref[...], preferred_element_type=F32)
        a3 = jnp.dot(x, w3b_ref[...], preferred_element_type=F32)
        a = (a1 * (1.0 / (1.0 + jnp.exp(-a1))) * a3).astype(BF16)
        ys_ref[r0:r0 + rows, :] = jnp.dot(a, w2b_ref[...], preferred_element_type=F32)

    def zero_rows(r0, rows):
        ys_ref[r0:r0 + rows, :] = jnp.zeros((rows, ys_ref.shape[1]), F32)

    @pl.when(nv > half)
    def _():
        swiglu_rows(0, tm)

    @pl.when((nv > 0) & (nv <= half))
    def _():
        swiglu_rows(0, half)
        zero_rows(half, half)

    @pl.when(nv == 0)
    def _():
        zero_rows(0, tm)


def _experts(tile_expert, tile_valid, n_used, next_expert, xs3, w1, w3, w2, *, layer, tm, n_tiles):
    d, de = w1.shape[2], w1.shape[3]
    hbm = pl.BlockSpec(memory_space=pl.ANY)
    grid_spec = pltpu.PrefetchScalarGridSpec(
        num_scalar_prefetch=4,
        grid=(n_tiles,),
        in_specs=[pl.BlockSpec((tm, d), lambda i, te, nv, nu, nx: (jnp.minimum(i, nu[0] - 1), 0)), hbm, hbm, hbm],
        out_specs=pl.BlockSpec((tm, d), lambda i, te, nv, nu, nx: (jnp.minimum(i, nu[0]), 0)),
        scratch_shapes=[pltpu.VMEM((d, de), F32), pltpu.VMEM((d, de), F32), pltpu.VMEM((de, d), F32),
                        pltpu.VMEM((d, de), BF16), pltpu.VMEM((d, de), BF16), pltpu.VMEM((de, d), BF16),
                        pltpu.SemaphoreType.DMA((3,))],
    )
    return pl.pallas_call(
        functools.partial(_expert_kernel, layer=layer),
        grid_spec=grid_spec,
        out_shape=jax.ShapeDtypeStruct(xs3.shape, F32),
        compiler_params=_params(1),
        name="experts",
    )(tile_expert, tile_valid, n_used, next_expert, xs3, w1, w3, w2)


def _combine_kernel(pos_ref, pos_next_ref, x1_ref, rw_ref, gfin_ref, ys_ref, o_ref, buf_ref, sem, *, tm):
    i = pl.program_id(0)
    slot = i % 2

    def gather(p_ref, s):
        def start(t, carry):
            for k in range(TOP_K):
                pos = p_ref[k * tm + t]
                pltpu.make_async_copy(ys_ref.at[pl.ds(pos, 1), :],
                                      buf_ref.at[s, k, pl.ds(t, 1), :], sem.at[s]).start()
            return carry
        lax.fori_loop(0, tm, start, 0, unroll=ISSUE_UNROLL)

    @pl.when(i == 0)
    def _():
        gather(pos_ref, 0)

    @pl.when(i + 1 < pl.num_programs(0))
    def _():
        gather(pos_next_ref, 1 - slot)

    for k in range(TOP_K):
        pltpu.make_async_copy(ys_ref.at[pl.ds(0, tm), :], buf_ref.at[slot, k], sem.at[slot]).wait()

    x = x1_ref[...]
    for k in range(TOP_K):
        x = x + rw_ref[:, k:k + 1] * buf_ref[slot, k]
    o_ref[...] = _rms(x, gfin_ref[...])


def _combine(pos_flat, x1, rw_cols, g_final, ys3, *, tm, row_offset, rows):
    d = x1.shape[1]
    off = row_offset // tm
    last = off + rows // tm - 1
    return pl.pallas_call(
        functools.partial(_combine_kernel, tm=tm),
        grid=(rows // tm,),
        in_specs=[
            pl.BlockSpec((TOP_K * tm,), lambda i: (i + off,), memory_space=pltpu.SMEM),
            pl.BlockSpec((TOP_K * tm,), lambda i: (jnp.minimum(i + off + 1, last),), memory_space=pltpu.SMEM),
            pl.BlockSpec((tm, d), lambda i: (i + off, 0)),
            pl.BlockSpec((tm, TOP_K), lambda i: (i + off, 0)),
            pl.BlockSpec((1, d), lambda i: (0, 0)),
            pl.BlockSpec(memory_space=pl.ANY),
        ],
        out_specs=pl.BlockSpec((tm, d), lambda i: (i, 0)),
        out_shape=jax.ShapeDtypeStruct((rows, d), F32),
        scratch_shapes=[pltpu.VMEM((2, TOP_K, tm, d), F32), pltpu.SemaphoreType.DMA((2,))],
        compiler_params=_params(1),
        name="combine",
    )(pos_flat, pos_flat, x1, rw_cols, g_final, ys3)


def _moe(x1, h_norm, logits, w1, w3, w2, g_final, *, layer, n_groups, epg, exp_row0, tm_route, tm_row, tm_exp):
    n_tok = x1.shape[0]
    n_exp = n_groups * epg
    ri, rw, cnt = _route(logits, tm=tm_route, n_groups=n_groups, epg=epg, exp_row0=exp_row0)
    counts = cnt[:, 0].astype(jnp.int32)
    n_tiles = (n_tok * TOP_K) // tm_exp + n_exp
    tiles_e = (counts + tm_exp - 1) // tm_exp
    tile_end = jnp.cumsum(tiles_e)
    tile_start = tile_end - tiles_e
    n_used = tile_end[-1]
    tile_id = jnp.arange(n_tiles, dtype=jnp.int32)
    tile_e = jnp.sum((tile_id[:, None] >= tile_end[None, :]).astype(jnp.int32), axis=1)
    tile_e = jnp.minimum(tile_e, n_exp - 1)
    tile_e = jnp.where(tile_id < n_used, tile_e, tile_e[jnp.maximum(n_used - 1, 0)])
    tile_valid = jnp.clip(counts[tile_e] - (tile_id - tile_start[tile_e]) * tm_exp, 0, tm_exp)
    tile_valid = jnp.where(tile_id < n_used, tile_valid, 0).astype(jnp.int32)
    group_end = tile_end[tile_e]
    next_expert = jnp.where(group_end < n_used, tile_e[jnp.minimum(group_end, n_tiles - 1)], -1).astype(jnp.int32)
    offsets = (tile_start * tm_exp).astype(jnp.int32)
    same = ri[:TOP_K, :, None] == jnp.arange(n_exp, dtype=jnp.int32)
    pos = jnp.sum(jnp.where(same, offsets, 0), axis=-1) + ri[TOP_K:2 * TOP_K]
    pos_flat = pos.reshape(TOP_K, n_tok // tm_row, tm_row).transpose(1, 0, 2).reshape(-1)
    rw_cols = rw[:TOP_K].T

    xs3 = _dispatch(pos_flat, h_norm, tm=tm_row, n_rows=n_tiles * tm_exp)
    ys3 = _experts(tile_e, tile_valid, n_used.reshape(1).astype(jnp.int32), next_expert, xs3, w1, w3, w2,
                   layer=layer, tm=tm_exp, n_tiles=n_tiles)
    return functools.partial(_combine, pos_flat, x1, rw_cols, g_final.reshape(1, -1), ys3, tm=tm_row)


def kernel(x_prompt, x_sample, cache_k, cache_v, page_table, g_attn, w_in, lambda_q1, lambda_k1, lambda_q2,
           lambda_k2, g_subln, ln_v_g, ln_v_b, w_spatial, b_spatial, w_out, g_ffn, w_router_grp, b_router_grp,
           w_router_exp, b_router_exp, w1, w3, w2, g_final):
    batch, seq, d = x_prompt.shape
    dec_batch, t, _ = x_sample.shape
    depth, n_phys, page, heads, _, head_dim = cache_k.shape
    groups, chunk = w_spatial.shape[1], w_spatial.shape[2]
    n_groups, n_exp = w_router_grp.shape[2], w_router_exp.shape[2]
    epg = n_exp // n_groups
    n_pages = page_table.shape[1]
    past = n_pages * page
    sec = w_in.shape[2] // 5
    n_p, n_s = batch * seq, dec_batch * t
    n_tok = n_p + n_s
    assert depth == 1 and head_dim == LANES
    assert sec == heads * 2 * head_dim == groups * chunk and chunk % t == 0

    l = 0
    lam_init = 0.8 - 0.6 * math.exp(-0.3 * l)
    tm_in, tq, tm_mix, tm_route, tm_row, tm_exp = 256, 256, 256, 512, 512, 512

    xp = x_prompt.reshape(n_p, d)
    xs = x_sample.reshape(n_s, d)
    row = lambda a: a[l].reshape(1, -1)
    lams = [row(a) for a in (lambda_q1, lambda_k1, lambda_q2, lambda_k2)]
    w_in_bf = w_in[l].astype(BF16)
    w_out_bf = w_out[l].astype(BF16)

    cos_p, sin_p = _rope_tables(seq, seq, 0, head_dim)
    cos_s, sin_s = _rope_tables(tm_in, t, past, head_dim)
    proj = functools.partial(_in_projection, g=row(g_attn), w_bf=w_in_bf, ln_g=row(ln_v_g), ln_b=row(ln_v_b),
                             tm=tm_in, heads=heads, head_dim=head_dim)
    q_p, k8_p, kb_p, v_p, vb_p, u_p, vg_p = proj(xp, cos_t=cos_p, sin_t=sin_p, table_blocks=seq // tm_in,
                                                 name="in_proj_prompt")
    q_s, k8_s, _, v_s, _, u_s, vg_s = proj(xs, cos_t=cos_s, sin_t=sin_s, table_blocks=1, name="in_proj_sample")

    att_p = _prompt_attention(lams, g_subln[l].reshape(-1, 1), q_p, kb_p, vb_p, batch=batch, seq=seq, heads=heads,
                              head_dim=head_dim, tq=tq, lam_init=lam_init)
    cache_k_rows = cache_k.reshape(-1, head_dim)
    cache_v_rows = cache_v.reshape(depth, n_phys, page, heads, 2, head_dim).transpose(0, 1, 2, 4, 3, 5)
    att_s = _sample_attention(page_table, lams, row(g_subln), q_s, k8_s, v_s, cache_k_rows,
                              cache_v_rows.reshape(-1, head_dim), layer=l, n_phys=n_phys, page=page,
                              dec_batch=dec_batch, t=t, heads=heads, head_dim=head_dim, lam_init=lam_init)

    exp_row0 = 8
    wr = jnp.zeros((d, LANES), F32).at[:, :n_groups].set(w_router_grp[l])
    wr = wr.at[:, exp_row0:exp_row0 + n_exp].set(w_router_exp[l])
    wr_hi = wr.astype(BF16)
    wr = jnp.stack([wr_hi, (wr - wr_hi.astype(F32)).astype(BF16)])
    br = jnp.zeros((1, LANES), F32).at[0, :n_groups].set(b_router_grp[l])
    br = br.at[0, exp_row0:exp_row0 + n_exp].set(b_router_exp[l])
    reps = chunk // t
    wm_s = jnp.tile(w_spatial[l][:, :t, :t], (1, reps, reps))
    bs_s = jnp.tile(b_spatial[l][:, :t], (1, reps))[..., None]
    mix = functools.partial(_mix_out, wo_bf=w_out_bf, g_ffn=row(g_ffn), wr=wr, br=br, n_total=n_tok, chunk=chunk)
    bufs = mix(xp, att_p, u_p, vg_p, w_spatial[l], b_spatial[l][..., None], prev=None, tm=tm_mix, row_offset=0,
               blk=chunk, name="mix_out_prompt")
    x1, h_norm, logits = mix(xs, att_s, u_s, vg_s, wm_s, bs_s, prev=bufs, tm=tm_mix, row_offset=n_p, blk=t,
                         name="mix_out_sample")

    comb = _moe(x1, h_norm, logits, w1, w3, w2, g_final, layer=l, n_groups=n_groups, epg=epg, exp_row0=exp_row0,
                tm_route=tm_route, tm_row=tm_row, tm_exp=tm_exp)
    y_p = comb(row_offset=0, rows=n_p)
    y_s = comb(row_offset=n_p, rows=n_s)

    kv6 = lambda a, b_: a.reshape(depth, b_, -1, heads, 2, head_dim)
    v5 = lambda a, b_: a.reshape(depth, b_, -1, heads, 2 * head_dim)
    return (y_p.reshape(batch, seq, d), y_s.reshape(dec_batch, t, d),
            kv6(k8_p, batch), v5(v_p, batch), kv6(k8_s, dec_batch), v5(v_s, dec_batch),
            vg_p.reshape(batch, seq, -1)[None, :, seq - chunk:], vg_s.reshape(depth, dec_batch, t, -1))
```
